```python
import jax, jax.numpy as jnp
from jax import lax
import numpy as np

D_MODEL = 1024
BATCH = 8
SEQ = 2048
DEPTH = 1
DEC_BATCH = 128
DEC_SEQ = 8
PAST_LEN = 8192
PAGE_SIZE = 128

HEAD_DIM = 64
D_ATTN = D_MODEL // 2
N_HEADS = D_ATTN // HEAD_DIM
N_KV_HEADS = N_HEADS // 4
GQA_GROUP = N_HEADS // N_KV_HEADS
KV_DIM = N_KV_HEADS * HEAD_DIM
N_BRANCH = 3
CMP_LEN = 32
CMP_STRIDE = 16
CMP_HIDDEN = HEAD_DIM
SEL_BLOCK = 64
N_SEL = 16
WINDOW = 512
ROT_DIM = HEAD_DIM // 4
ROPE_THETA = 500000.0
D_CONV = D_MODEL - D_ATTN
CONV_W = 3
D_MIX = D_ATTN + D_CONV
D_FF = ((8 * D_MODEL + 3 * 256 - 1) // (3 * 256)) * 256
N_IN = D_ATTN + 2 * N_BRANCH * KV_DIM + N_BRANCH * N_HEADS + 3 * D_CONV
Q_BLOCK = 64
EPS = 1e-6
NEG_INF = -1e30
SEL_BONUS = 1e3

kernel_name = 'hymba_nsa_shortconv_adaln_step'


def rmsnorm(x, g):
    xf = x.astype(jnp.float32)
    y = xf * lax.rsqrt(jnp.mean(xf * xf, axis=-1, keepdims=True) + EPS)
    return (y * g.astype(jnp.float32)).astype(x.dtype)


def rope(x, pos):
    half = ROT_DIM // 2
    inv = ROPE_THETA ** (-jnp.arange(half, dtype=jnp.float32) * 2.0 / ROT_DIM)
    ang = pos.astype(jnp.float32)[:, None] * inv[None, :]
    cos = jnp.cos(ang)[:, None, :].astype(x.dtype)
    sin = jnp.sin(ang)[:, None, :].astype(x.dtype)
    x1, x2 = x[..., :half], x[..., half:ROT_DIM]
    return jnp.concatenate([x1 * cos - x2 * sin, x2 * cos + x1 * sin, x[..., ROT_DIM:]], axis=-1)


def masked_softmax(s, mask):
    s = jnp.where(mask, s, NEG_INF)
    m = jnp.max(s, axis=-1, keepdims=True)
    e = jnp.where(mask, jnp.exp(s - m), 0.0)
    return e / jnp.maximum(jnp.sum(e, axis=-1, keepdims=True), 1e-30)


def adaln(c, w_ada, b_ada):
    mod = jax.nn.silu(c) @ w_ada + b_ada
    return jnp.split(mod[:, None, :], 6, axis=-1)


def modulate(x, g, shift, scale):
    return rmsnorm(x, g) * (1.0 + scale) + shift


def project(h, pos, w_in, q_norm_g, k_norm_g):
    b, t, _ = h.shape
    sizes = [D_ATTN] + [KV_DIM] * 6 + [N_BRANCH * N_HEADS] + [D_CONV] * 3
    offs = np.cumsum(sizes)[:-1].tolist()
    q, kc, vc, ks, vs, kw, vw, g, hc, bg, cg = jnp.split(h @ w_in, offs, axis=-1)
    heads = lambda z: z.reshape(b, t, -1, HEAD_DIM)
    q = rope(rmsnorm(heads(q), q_norm_g), pos)
    ks = rope(rmsnorm(heads(ks), k_norm_g[1]), pos)
    kw = rope(rmsnorm(heads(kw), k_norm_g[2]), pos)
    gates = jax.nn.sigmoid(g).reshape(b, t, N_HEADS, N_BRANCH)
    u = cg * hc
    return q, heads(kc), heads(vc), ks, heads(vs), kw, heads(vw), gates, u, bg


def compress(rows, pe, w1, w2):
    b, t = rows.shape[:2]
    r_n = CMP_LEN // CMP_STRIDE
    n_ch = t // CMP_STRIDE
    nc = n_ch - r_n + 1
    chunks = rows[:, :n_ch * CMP_STRIDE].reshape(b, n_ch, CMP_STRIDE, N_KV_HEADS, HEAD_DIM)
    chunks = chunks.transpose(0, 1, 3, 2, 4).reshape(b, n_ch, N_KV_HEADS, CMP_STRIDE * HEAD_DIM)
    proj = jnp.einsum('bnkf,rfh->rbnkh', chunks, w1.reshape(r_n, CMP_STRIDE * HEAD_DIM, CMP_HIDDEN))
    pre = sum(proj[r][:, r:r + nc] for r in range(r_n)) + pe.reshape(-1) @ w1
    out = jax.nn.silu(pre) @ w2
    end = jnp.arange(nc) * CMP_STRIDE + CMP_LEN - 1
    return out, end


def compressed_tokens(kc_raw, vc_raw, k_gain, pe, w_ck1, w_ck2, w_cv1, w_cv2):
    kc, end = compress(kc_raw, pe, w_ck1, w_ck2)
    vc, _ = compress(vc_raw, pe, w_cv1, w_cv2)
    return rope(rmsnorm(kc, k_gain), end), vc, end


def to_blocks(rows):
    b, t = rows.shape[:2]
    ns = -(-t // SEL_BLOCK)
    rows = jnp.pad(rows, ((0, 0), (0, ns * SEL_BLOCK - t), (0, 0), (0, 0)))
    return rows.reshape(b, ns, SEL_BLOCK, N_KV_HEADS, HEAD_DIM).transpose(0, 3, 1, 2, 4)


def nsa_attend(q, q_pos, kc, vc, kc_end, ksb, vsb, kw, vw, kw_pos, gates):
    b, nq = q.shape[:2]
    scale = HEAD_DIM ** -0.5
    qh = q.reshape(b, nq, N_KV_HEADS, GQA_GROUP, HEAD_DIM).transpose(0, 2, 3, 1, 4)
    s = jnp.einsum('bkgqd,bnkd->bkgqn', qh, kc).astype(jnp.float32) * scale
    p_cmp = masked_softmax(s, kc_end[None, :] <= q_pos[:, None])
    o_cmp = jnp.einsum('bkgqn,bnkd->bkgqd', p_cmp.astype(vc.dtype), vc)
    ns = ksb.shape[2]
    blk = jnp.arange(ns)
    c_start = kc_end - (CMP_LEN - 1)
    overlap = ((c_start[:, None] < (blk[None, :] + 1) * SEL_BLOCK)
               & (c_start[:, None] + CMP_LEN > blk[None, :] * SEL_BLOCK)).astype(jnp.float32)
    imp = jnp.einsum('bkgqn,nj->bkqj', p_cmp, overlap)
    cur = (q_pos // SEL_BLOCK)[:, None]
    forced = (blk[None, :] == 0) | (blk[None, :] == cur) | (blk[None, :] == cur - 1)
    causal_blk = blk[None, :] * SEL_BLOCK <= q_pos[:, None]
    score = jnp.where(causal_blk, imp + SEL_BONUS * forced.astype(jnp.float32), NEG_INF)
    n_sel = min(N_SEL, ns)
    _, idx = lax.top_k(score, n_sel)
    take = jax.vmap(jax.vmap(lambda rows, i: rows[i]))
    k_sel = take(ksb, idx).reshape(b, N_KV_HEADS, nq, n_sel * SEL_BLOCK, HEAD_DIM)
    v_sel = take(vsb, idx).reshape(b, N_KV_HEADS, nq, n_sel * SEL_BLOCK, HEAD_DIM)
    pos_sel = (idx[..., None] * SEL_BLOCK + jnp.arange(SEL_BLOCK)).reshape(b, N_KV_HEADS, nq, n_sel * SEL_BLOCK)
    s = jnp.einsum('bkgqd,bkqmd->bkgqm', qh, k_sel).astype(jnp.float32) * scale
    p_slc = masked_softmax(s, (pos_sel <= q_pos[:, None])[:, :, None])
    o_slc = jnp.einsum('bkgqm,bkqmd->bkgqd', p_slc.astype(v_sel.dtype), v_sel)
    s = jnp.einsum('bkgqd,blkd->bkgql', qh, kw).astype(jnp.float32) * scale
    dist = q_pos[:, None] - kw_pos[None, :]
    p_win = masked_softmax(s, (dist >= 0) & (dist < WINDOW) & (kw_pos[None, :] >= 0))
    o_win = jnp.einsum('bkgql,blkd->bkgqd', p_win.astype(vw.dtype), vw)
    o = jnp.stack([o_cmp, o_slc, o_win], axis=-1)
    o = o.transpose(0, 3, 1, 2, 4, 5).reshape(b, nq, N_HEADS, HEAD_DIM, N_BRANCH)
    return jnp.einsum('bqhdc,bqhc->bqhd', o, gates.astype(o.dtype))


def short_conv(u, prev, w_conv):
    ext = jnp.concatenate([prev, u], axis=1)
    t = u.shape[1]
    y = sum(ext[:, k:k + t] * w_conv[k] for k in range(CONV_W))
    return y, ext[:, -(CONV_W - 1):]


def mix_out(o_attn, conv_y, attn_out_g, conv_out_g, w_o):
    b, t = conv_y.shape[:2]
    a = rmsnorm(o_attn.reshape(b, t, D_ATTN), attn_out_g)
    v = rmsnorm(conv_y, conv_out_g)
    return jnp.concatenate([a, v], axis=-1) @ w_o


def swiglu(h, w_gu, w_down):
    g, u = jnp.split(h @ w_gu, 2, axis=-1)
    return (jax.nn.silu(g) * u) @ w_down


def gather_past(pool, page_table, new_rows):
    b = page_table.shape[0]
    old = pool[page_table].reshape(b, -1, N_KV_HEADS, HEAD_DIM)
    return jnp.concatenate([old, new_rows], axis=1)


def setup_inputs(seed: int = 0) -> dict:
    key = jax.random.key(seed)
    k = jax.random.split(key, 32)
    n_pages = PAST_LEN // PAGE_SIZE
    n_pool = (5 * DEC_BATCH * n_pages + 3) // 4
    w_buf = min(WINDOW, PAST_LEN)

    def nrm(kk, shape, s=1.0):
        return jax.random.normal(kk, shape, jnp.float32) * s

    def gain(kk, shape):
        return 1.0 + 0.02 * jax.random.normal(kk, shape, jnp.float32)

    page_table = jax.random.permutation(k[0], n_pool)[:DEC_BATCH * n_pages].reshape(DEC_BATCH, n_pages).astype(jnp.int32)
    pool = (DEPTH, n_pool, PAGE_SIZE, N_KV_HEADS, HEAD_DIM)
    return {
        'x_prompt': nrm(k[1], (BATCH, SEQ, D_MODEL)),
        'x_sample': nrm(k[2], (DEC_BATCH, DEC_SEQ, D_MODEL)),
        'cache_k_cmp': nrm(k[3], pool),
        'cache_v_cmp': nrm(k[4], pool),
        'cache_k_slc': nrm(k[5], pool),
        'cache_v_slc': nrm(k[6], pool),
        'state_k_win': nrm(k[7], (DEPTH, DEC_BATCH, w_buf, N_KV_HEADS, HEAD_DIM)),
        'state_v_win': nrm(k[8], (DEPTH, DEC_BATCH, w_buf, N_KV_HEADS, HEAD_DIM)),
        'state_conv': nrm(k[9], (DEPTH, DEC_BATCH, CONV_W - 1, D_CONV)),
        'page_table': page_table,
        'c_prompt': nrm(k[10], (BATCH, D_MODEL)),
        'c_sample': nrm(k[11], (DEC_BATCH, D_MODEL)),
        'w_ada': nrm(k[12], (DEPTH, D_MODEL, 6 * D_MODEL), 0.5 * D_MODEL ** -0.5),
        'b_ada': nrm(k[13], (DEPTH, 6 * D_MODEL), 0.01),
        'ln1_g': gain(k[14], (DEPTH, D_MODEL)),
        'w_in': nrm(k[15], (DEPTH, D_MODEL, N_IN), D_MODEL ** -0.5),
        'q_norm_g': gain(k[16], (DEPTH, HEAD_DIM)),
        'k_norm_g': gain(k[17], (DEPTH, N_BRANCH, HEAD_DIM)),
        'pe_cmp': nrm(k[18], (DEPTH, CMP_LEN, HEAD_DIM), 0.1),
        'w_ck1': nrm(k[19], (DEPTH, CMP_LEN * HEAD_DIM, CMP_HIDDEN), (CMP_LEN * HEAD_DIM) ** -0.5),
        'w_ck2': nrm(k[20], (DEPTH, CMP_HIDDEN, HEAD_DIM), CMP_HIDDEN ** -0.5),
        'w_cv1': nrm(k[21], (DEPTH, CMP_LEN * HEAD_DIM, CMP_HIDDEN), (CMP_LEN * HEAD_DIM) ** -0.5),
        'w_cv2': nrm(k[22], (DEPTH, CMP_HIDDEN, HEAD_DIM), CMP_HIDDEN ** -0.5),
        'w_conv': nrm(k[23], (DEPTH, CONV_W, D_CONV), CONV_W ** -0.5),
        'attn_out_g': gain(k[24], (DEPTH, D_ATTN)),
        'conv_out_g': gain(k[25], (DEPTH, D_CONV)),
        'w_o': nrm(k[26], (DEPTH, D_MIX, D_MODEL), D_MIX ** -0.5),
        'ln2_g': gain(k[27], (DEPTH, D_MODEL)),
        'w_gu': nrm(k[28], (DEPTH, D_MODEL, 2 * D_FF), D_MODEL ** -0.5),
        'w_down': nrm(k[29], (DEPTH, D_FF, D_MODEL), D_FF ** -0.5),
    }


def reference(x_prompt, x_sample, cache_k_cmp, cache_v_cmp, cache_k_slc, cache_v_slc, state_k_win, state_v_win,
              state_conv, page_table, c_prompt, c_sample, w_ada, b_ada, ln1_g, w_in, q_norm_g, k_norm_g, pe_cmp,
              w_ck1, w_ck2, w_cv1, w_cv2, w_conv, attn_out_g, conv_out_g, w_o, ln2_g, w_gu, w_down):
    bp, tp = x_prompt.shape[:2]
    ts = x_sample.shape[1]
    past = page_table.shape[1] * cache_k_cmp.shape[2]
    wbuf = state_k_win.shape[2]
    pos_p = jnp.arange(tp)
    pos_s = past + jnp.arange(ts)
    xp, xs = x_prompt, x_sample
    per_layer = []
    for l in range(DEPTH):
        sh1, sc1, g1, sh2, sc2, g2 = adaln(c_prompt, w_ada[l], b_ada[l])
        q, kc_r, vc_r, ks, vs, kw, vw, gates, u, bg = project(modulate(xp, ln1_g[l], sh1, sc1), pos_p,
                                                              w_in[l], q_norm_g[l], k_norm_g[l])
        kc, vc, kc_end = compressed_tokens(kc_r, vc_r, k_norm_g[l, 0], pe_cmp[l], w_ck1[l], w_ck2[l], w_cv1[l], w_cv2[l])
        ksb, vsb = to_blocks(ks), to_blocks(vs)
        kwp = jnp.pad(kw, ((0, 0), (WINDOW, 0), (0, 0), (0, 0)))
        vwp = jnp.pad(vw, ((0, 0), (WINDOW, 0), (0, 0), (0, 0)))

        def prompt_block(s0):
            qc = lax.dynamic_slice_in_dim(q, s0, Q_BLOCK, 1)
            gc = lax.dynamic_slice_in_dim(gates, s0, Q_BLOCK, 1)
            kwc = lax.dynamic_slice_in_dim(kwp, s0, WINDOW + Q_BLOCK, 1)
            vwc = lax.dynamic_slice_in_dim(vwp, s0, WINDOW + Q_BLOCK, 1)
            q_pos = s0 + jnp.arange(Q_BLOCK)
            kw_pos = s0 - WINDOW + jnp.arange(WINDOW + Q_BLOCK)
            return nsa_attend(qc, q_pos, kc, vc, kc_end, ksb, vsb, kwc, vwc, kw_pos, gc)

        o = lax.map(prompt_block, jnp.arange(0, tp, Q_BLOCK))
        o = o.transpose(1, 0, 2, 3, 4).reshape(bp, tp, N_HEADS, HEAD_DIM)
        cy, p_conv_l = short_conv(u, jnp.zeros((bp, CONV_W - 1, D_CONV), u.dtype), w_conv[l])
        xp = xp + g1 * mix_out(o, bg * cy, attn_out_g[l], conv_out_g[l], w_o[l])
        xp = xp + g2 * swiglu(modulate(xp, ln2_g[l], sh2, sc2), w_gu[l], w_down[l])
        pw = min(WINDOW, tp)
        prompt_state = (kc_r, vc_r, ks, vs, kw[:, tp - pw:], vw[:, tp - pw:], p_conv_l)

        sh1, sc1, g1, sh2, sc2, g2 = adaln(c_sample, w_ada[l], b_ada[l])
        q, kc_r, vc_r, ks, vs, kw, vw, gates, u, bg = project(modulate(xs, ln1_g[l], sh1, sc1), pos_s,
                                                              w_in[l], q_norm_g[l], k_norm_g[l])
        kc, vc, kc_end = compressed_tokens(gather_past(cache_k_cmp[l], page_table, kc_r),
                                           gather_past(cache_v_cmp[l], page_table, vc_r),
                                           k_norm_g[l, 0], pe_cmp[l], w_ck1[l], w_ck2[l], w_cv1[l], w_cv2[l])
        ksb = to_blocks(gather_past(cache_k_slc[l], page_table, ks))
        vsb = to_blocks(gather_past(cache_v_slc[l], page_table, vs))
        kw_all = jnp.concatenate([state_k_win[l], kw], axis=1)
        vw_all = jnp.concatenate([state_v_win[l], vw], axis=1)
        kw_pos = past - wbuf + jnp.arange(wbuf + ts)
        o = nsa_attend(q, pos_s, kc, vc, kc_end, ksb, vsb, kw_all, vw_all, kw_pos, gates)
        cy, s_conv_l = short_conv(u, state_conv[l], w_conv[l])
        xs = xs + g1 * mix_out(o, bg * cy, attn_out_g[l], conv_out_g[l], w_o[l])
        xs = xs + g2 * swiglu(modulate(xs, ln2_g[l], sh2, sc2), w_gu[l], w_down[l])
        sample_state = (kc_r, vc_r, ks, vs, kw_all[:, ts:], vw_all[:, ts:], s_conv_l)
        per_layer.append(prompt_state + sample_state)

    (p_k_cmp, p_v_cmp, p_k_slc, p_v_slc, p_k_win, p_v_win, p_conv,
     s_k_cmp, s_v_cmp, s_k_slc, s_v_slc, s_k_win, s_v_win, s_conv) = [jnp.stack(z) for z in zip(*per_layer)]
    return (xp, xs, p_k_cmp, p_v_cmp, p_k_slc, p_v_slc, p_k_win, p_v_win, p_conv,
            s_k_cmp, s_v_cmp, s_k_slc, s_v_slc, s_k_win, s_v_win, s_conv)
```

```python
import functools

import numpy as np
import jax
import jax.numpy as jnp
from jax import lax
from jax.experimental import pallas as pl
from jax.experimental.pallas import tpu as pltpu

F32 = jnp.float32
BF16 = jnp.bfloat16

D_MODEL = 1024
HEAD_DIM = 64
N_HEADS = 8
N_KV_HEADS = 2
GQA_GROUP = 4
KV_DIM = N_KV_HEADS * HEAD_DIM
D_ATTN = N_HEADS * HEAD_DIM
D_CONV = D_MODEL - D_ATTN
N_BRANCH = 3
CMP_LEN = 32
CMP_STRIDE = 16
SEL_BLOCK = 64
N_SEL = 16
WINDOW = 512
ROT_DIM = HEAD_DIM // 4
ROPE_THETA = 500000.0
CONV_W = 3
D_FF = 2816
EPS = 1e-6
NEG_INF = -1e30
SEL_BONUS = 1e3
PAD_SCORE = -3e38

LANES = 128
N_IN_PAD = D_ATTN + 6 * KV_DIM + 3 * D_CONV + LANES
VMEM_LIMIT = 56 * 1024 * 1024


def _dot(a, b):
    return jnp.dot(a, b, preferred_element_type=F32)


def _dot_t(a, b):
    return lax.dot_general(a, b, (((1,), (1,)), ((), ())), preferred_element_type=F32)


def _rms(x):
    return x * lax.rsqrt(jnp.mean(x * x, axis=-1, keepdims=True) + EPS)


def _silu(x):
    return x * jax.nn.sigmoid(x)


def _head_rms(z, p128):
    ms = _dot((z * z).astype(BF16), p128)
    return z * lax.rsqrt(ms + EPS)


def _rope(z, cos, sa, sb):
    return z * cos + pltpu.roll(z, 8, 1) * sa + pltpu.roll(z, LANES - 8, 1) * sb


def _masked_softmax(s, mask):
    sm = jnp.where(mask, s, NEG_INF)
    m = jnp.max(sm, axis=-1, keepdims=True)
    e = jnp.where(mask, jnp.exp(sm - m), 0.0)
    return e / jnp.maximum(jnp.sum(e, axis=-1, keepdims=True), 1e-30)


def _adaln_kernel(c_ref, w_ref, b_ref, o_ref):
    a = _silu(c_ref[...]).astype(BF16)
    o_ref[...] = _dot(a, w_ref[...].astype(BF16)) + b_ref[...]


def _adaln_call(c_all, w_ada, b_ada):
    n, d = c_all.shape
    n_out = w_ada.shape[1]
    bn = 512
    return pl.pallas_call(
        _adaln_kernel,
        grid=(n_out // bn,),
        in_specs=[pl.BlockSpec((n, d), lambda j: (0, 0)),
                  pl.BlockSpec((d, bn), lambda j: (0, j)),
                  pl.BlockSpec((1, bn), lambda j: (0, j))],
        out_specs=pl.BlockSpec((n, bn), lambda j: (0, j)),
        out_shape=jax.ShapeDtypeStruct((n, n_out), F32),
        compiler_params=pltpu.CompilerParams(dimension_semantics=("arbitrary",),
                                             vmem_limit_bytes=VMEM_LIMIT),
        name="adaln",
    )(c_all, w_ada, b_ada.reshape(1, n_out))


def _inproj_kernel(*refs, G, L, tiles_per_group, has_prev):
    (x_ref, shift_ref, scale_ref, ln1_ref, w_ref, qg_ref, ksg_ref, kwg_ref, p128_ref,
     cos_ref, sa_ref, sb_ref, wconv_ref, convg_ref) = refs[:14]
    pos = 14
    prev_ref = None
    if has_prev:
        prev_ref = refs[pos]
        pos += 1
    q_o = refs[pos]
    state_o = refs[pos + 1:pos + 7]
    copy_o = refs[pos + 7:pos + 13]
    gate_o, vconv_o, cstate_o = refs[pos + 13:pos + 16]
    carry_ref = None if has_prev else refs[pos + 16]
    native_states = not has_prev
    M = G * L

    def expand(v):
        return jnp.broadcast_to(v, (G, L, v.shape[-1])).reshape(M, v.shape[-1])

    x = x_ref[...]
    xn = _rms(x) * ln1_ref[...]
    h = (xn * (1.0 + expand(scale_ref[...])) + expand(shift_ref[...])).astype(BF16)

    p128 = p128_ref[...]
    cos = expand(cos_ref[...])
    sa = expand(sa_ref[...])
    sb = expand(sb_ref[...])

    zq = _dot(h, w_ref[:, 0:D_ATTN])
    for c in range(D_ATTN // LANES):
        sl = slice(c * LANES, (c + 1) * LANES)
        z = _head_rms(zq[:, sl], p128) * qg_ref[:, sl]
        q_o[:, sl] = (_rope(z, cos, sa, sb) * (HEAD_DIM ** -0.5)).astype(BF16)

    zkv = _dot(h, w_ref[:, D_ATTN:D_ATTN + 6 * KV_DIM])
    ks = _rope(_head_rms(zkv[:, 256:384], p128) * ksg_ref[...], cos, sa, sb)
    kw = _rope(_head_rms(zkv[:, 512:640], p128) * kwg_ref[...], cos, sa, sb)
    rows = (zkv[:, 0:128], zkv[:, 128:256], ks, zkv[:, 384:512], kw, zkv[:, 640:768])
    for r, st_o, b_o in zip(rows, state_o, copy_o):
        if native_states:
            st_o[0] = r.T
        else:
            st_o[...] = r
        b_o[...] = r.astype(BF16)

    c0 = D_ATTN + 6 * KV_DIM
    gate_o[...] = jax.nn.sigmoid(_dot(h, w_ref[:, c0 + 3 * D_CONV:c0 + 3 * D_CONV + LANES]))

    zc = _dot(h, w_ref[:, c0:c0 + 3 * D_CONV])
    hc = zc[:, 0:D_CONV]
    bg = zc[:, D_CONV:2 * D_CONV]
    cg = zc[:, 2 * D_CONV:3 * D_CONV]
    u = cg * hc
    if has_prev:
        prev = prev_ref[...]
    else:
        first = (pl.program_id(0) % tiles_per_group) == 0
        prev = jnp.where(first, 0.0, carry_ref[6:8, :])[None]
    prev_a = expand(prev[:, 0:1, :])
    prev_b = expand(prev[:, 1:2, :])
    t = lax.broadcasted_iota(jnp.int32, (M, 1), 0) & (L - 1)
    u1 = jnp.where(t == 0, prev_b, pltpu.roll(u, 1, 0))
    u2 = jnp.where(t == 0, prev_a, jnp.where(t == 1, prev_b, pltpu.roll(u, 2, 0)))
    cy = u2 * wconv_ref[0:1, :] + u1 * wconv_ref[1:2, :] + u * wconv_ref[2:3, :]
    vconv_o[...] = (_rms(bg * cy) * convg_ref[...]).astype(BF16)
    cstate_o[...] = u.reshape(G, L, D_CONV)[:, L - 2:L, :]
    if not has_prev:
        carry_ref[...] = u[M - 8:M, :]


def _inproj_call(x2, mod3, mod_off, G, L, tiles_per_group, tabs, prev3, wts):
    ntok = x2.shape[0]
    M = G * L
    n_tiles = ntok // M
    has_prev = prev3 is not None
    if has_prev:
        assert tiles_per_group == 1
        n_groups = ntok // L
        mod_idx = lambda i: i
        tab_idx = lambda i: 0
    else:
        assert G == 1
        n_groups = n_tiles // tiles_per_group
        mod_idx = lambda i: mod_off + i // tiles_per_group
        tab_idx = lambda i: i % tiles_per_group
    assert L & (L - 1) == 0
    cos, sa, sb = tabs
    w_in, qg, ksg, kwg, p128, wconv, convg, ln1 = wts

    const2 = lambda shape: pl.BlockSpec(shape, lambda i: (0, 0))
    tok = lambda n: pl.BlockSpec((M, n), lambda i: (i, 0))
    tabspec = pl.BlockSpec((1, L, LANES), lambda i: (0, tab_idx(i), 0))
    in_specs = [tok(D_MODEL),
                pl.BlockSpec((G, 1, D_MODEL), lambda i: (mod_idx(i), 0, 0)),
                pl.BlockSpec((G, 1, D_MODEL), lambda i: (mod_idx(i), 0, 1)),
                const2((1, D_MODEL)), const2((D_MODEL, N_IN_PAD)), const2((1, D_ATTN)),
                const2((1, KV_DIM)), const2((1, KV_DIM)), const2((LANES, LANES)),
                tabspec, tabspec, tabspec, const2((CONV_W, D_CONV)), const2((1, D_CONV))]
    args = [x2, mod3, mod3, ln1, w_in, qg, ksg, kwg, p128, cos, sa, sb, wconv, convg]
    if has_prev:
        in_specs.append(pl.BlockSpec((G, CONV_W - 1, D_CONV), lambda i: (i, 0, 0)))
        args.append(prev3)
    f32tok = lambda n: jax.ShapeDtypeStruct((ntok, n), F32)
    b16tok = lambda n: jax.ShapeDtypeStruct((ntok, n), BF16)
    if has_prev:
        state_sd, state_spec = f32tok(KV_DIM), tok(KV_DIM)
        cstate_spec = pl.BlockSpec((G, CONV_W - 1, D_CONV), lambda i: (i, 0, 0))
    else:
        state_sd = jax.ShapeDtypeStruct((n_groups, KV_DIM, tiles_per_group * L), F32)
        state_spec = pl.BlockSpec((1, KV_DIM, L), lambda i: (i // tiles_per_group, 0, i % tiles_per_group))
        cstate_spec = pl.BlockSpec((1, CONV_W - 1, D_CONV), lambda i: (i // tiles_per_group, 0, 0))
    out_shape = ([b16tok(D_ATTN)] + [state_sd] * 6 + [b16tok(KV_DIM)] * 6
                 + [f32tok(LANES), b16tok(D_CONV),
                    jax.ShapeDtypeStruct((n_groups, CONV_W - 1, D_CONV), F32)])
    out_specs = ([tok(D_ATTN)] + [state_spec] * 6 + [tok(KV_DIM)] * 6 + [tok(LANES), tok(D_CONV), cstate_spec])
    scratch = [] if has_prev else [pltpu.VMEM((8, D_CONV), F32)]
    return pl.pallas_call(
        functools.partial(_inproj_kernel, G=G, L=L, tiles_per_group=tiles_per_group,
                          has_prev=has_prev),
        grid=(n_tiles,),
        in_specs=in_specs, out_specs=out_specs, out_shape=out_shape,
        scratch_shapes=scratch,
        compiler_params=pltpu.CompilerParams(dimension_semantics=("arbitrary",),
                                             vmem_limit_bytes=VMEM_LIMIT),
        name="inproj_sample" if has_prev else "inproj_prompt",
    )(*args)


def _compress_math(chunks, w1, w2, pe_rows, n_ch):
    proj = _dot(chunks, w1)
    ppe = _dot(pe_rows, w1)
    bias = ppe[0:1, 0:KV_DIM] + ppe[1:2, KV_DIM:2 * KV_DIM]
    pre = proj[:, 0:KV_DIM] + pltpu.roll(proj[:, KV_DIM:2 * KV_DIM], n_ch - 1, 0) + bias
    return _dot(_silu(pre).astype(BF16), w2)


def _compress_finish_k(kc, kg_ref, p128_ref, cos_ref, sa_ref, sb_ref):
    return _rope(_head_rms(kc, p128_ref[...]) * kg_ref[...], cos_ref[...], sa_ref[...], sb_ref[...])


def _prompt_compress_kernel(kr_ref, vr_ref, w1k_ref, w2k_ref, w1v_ref, w2v_ref, pe_ref, kg_ref,
                            p128_ref, cos_ref, sa_ref, sb_ref, kc_o, vc_o, *, n_ch):
    pe = pe_ref[...]
    kc = _compress_math(kr_ref[...].astype(BF16), w1k_ref[...], w2k_ref[...], pe, n_ch)
    kc_o[...] = _compress_finish_k(kc, kg_ref, p128_ref, cos_ref, sa_ref, sb_ref).astype(BF16)
    vc_o[...] = _compress_math(vr_ref[...].astype(BF16), w1v_ref[...], w2v_ref[...], pe,
                               n_ch).astype(BF16)


def _prompt_compress_call(kc_r, vc_r, n_batch, cw, tabs):
    T = kc_r.shape[0] // n_batch
    n_ch = T // CMP_STRIDE
    feat = CMP_STRIDE * KV_DIM
    w1k, w2k, w1v, w2v, pe_rows, kg, p128 = cw
    cos, sa, sb = tabs
    const2 = lambda a: pl.BlockSpec(a.shape, lambda b: (0, 0))
    chunk_spec = pl.BlockSpec((n_ch, feat), lambda b: (b, 0))
    out_spec = pl.BlockSpec((n_ch, KV_DIM), lambda b: (b, 0))
    out_sd = jax.ShapeDtypeStruct((n_batch * n_ch, KV_DIM), BF16)
    return pl.pallas_call(
        functools.partial(_prompt_compress_kernel, n_ch=n_ch),
        grid=(n_batch,),
        in_specs=[chunk_spec, chunk_spec] + [const2(a) for a in
                                             (w1k, w2k, w1v, w2v, pe_rows, kg, p128, cos, sa, sb)],
        out_specs=[out_spec, out_spec], out_shape=[out_sd, out_sd],
        compiler_params=pltpu.CompilerParams(dimension_semantics=("arbitrary",),
                                             vmem_limit_bytes=VMEM_LIMIT),
        name="compress_prompt",
    )(kc_r.reshape(n_batch * n_ch, feat), vc_r.reshape(n_batch * n_ch, feat),
      w1k, w2k, w1v, w2v, pe_rows, kg, p128, cos, sa, sb)


def _page_gather(pt_ref, pools, bufs, sem, dst_of, n_pages):
    b = pl.program_id(0)
    nb = pl.num_programs(0)
    slot = b % 2

    def copy(i, pg, p, s):
        return pltpu.make_async_copy(pools[i].at[pg], dst_of(bufs[i], s, p), sem.at[i, s])

    def issue(bb, s):
        def body(p, carry):
            pg = pt_ref[bb, p]
            for i in range(len(pools)):
                copy(i, pg, p, s).start()
            return carry
        lax.fori_loop(0, n_pages, body, 0)

    @pl.when(b == 0)
    def _():
        issue(0, 0)

    @pl.when(b + 1 < nb)
    def _():
        issue(b + 1, 1 - slot)

    def wait_body(p, carry):
        for i in range(len(pools)):
            copy(i, 0, p, slot).wait()
        return carry
    lax.fori_loop(0, n_pages, wait_body, 0)
    return slot


def _sample_compress_kernel(pt_ref, kpool, vpool, w1k_ref, w2k_ref, w1v_ref, w2v_ref, pe_ref,
                            kg_ref, p128_ref, cos_ref, sa_ref, sb_ref, kc_o, vc_o,
                            kbuf, vbuf, rows, sem, *, n_pages, page):
    n_ch = n_pages * page // CMP_STRIDE
    slot = _page_gather(pt_ref, (kpool, vpool), (kbuf, vbuf), sem,
                        lambda buf, s, p: buf.at[s, p], n_pages)

    def chunk_rows(buf):
        def body(p, carry):
            rows[pl.ds(pl.multiple_of(p * page, page), page), :] = buf[slot, p].T
            return carry
        lax.fori_loop(0, n_pages, body, 0)
        return jnp.concatenate([rows[pl.ds(j, n_ch, stride=CMP_STRIDE), :].astype(BF16)
                                for j in range(CMP_STRIDE)], axis=1)

    pe = pe_ref[...]
    kc = _compress_math(chunk_rows(kbuf), w1k_ref[...], w2k_ref[...], pe, n_ch)
    kc_o[0] = _compress_finish_k(kc, kg_ref, p128_ref, cos_ref, sa_ref, sb_ref).astype(BF16)
    vc_o[0] = _compress_math(chunk_rows(vbuf), w1v_ref[...], w2v_ref[...], pe, n_ch).astype(BF16)


def _sample_compress_call(page_table, kpool_t, vpool_t, cw, tabs):
    n_batch, n_pages = page_table.shape
    page = kpool_t.shape[2]
    n_ch = n_pages * page // CMP_STRIDE
    w1k, w2k, w1v, w2v, pe_rows, kg, p128 = cw
    cos, sa, sb = tabs
    const2 = lambda a: pl.BlockSpec(a.shape, lambda b, pt: (0, 0))
    anyspec = pl.BlockSpec(memory_space=pl.ANY)
    out_spec = pl.BlockSpec((1, n_ch, KV_DIM), lambda b, pt: (b, 0, 0))
    out_sd = jax.ShapeDtypeStruct((n_batch, n_ch, KV_DIM), BF16)
    grid_spec = pltpu.PrefetchScalarGridSpec(
        num_scalar_prefetch=1, grid=(n_batch,),
        in_specs=[anyspec, anyspec] + [const2(a) for a in
                                       (w1k, w2k, w1v, w2v, pe_rows, kg, p128, cos, sa, sb)],
        out_specs=[out_spec, out_spec],
        scratch_shapes=[pltpu.VMEM((2, n_pages, KV_DIM, page), F32), pltpu.VMEM((2, n_pages, KV_DIM, page), F32),
                        pltpu.VMEM((n_pages * page, KV_DIM), F32), pltpu.SemaphoreType.DMA((2, 2))])
    return pl.pallas_call(
        functools.partial(_sample_compress_kernel, n_pages=n_pages, page=page),
        grid_spec=grid_spec, out_shape=[out_sd, out_sd],
        compiler_params=pltpu.CompilerParams(dimension_semantics=("arbitrary",),
                                             vmem_limit_bytes=VMEM_LIMIT),
        name="compress_sample",
    )(page_table, kpool_t, vpool_t, w1k, w2k, w1v, w2v, pe_rows, kg, p128, cos, sa, sb)


def _query_rows(q, kvh):
    lane = lax.broadcasted_iota(jnp.int32, (1, LANES), 1)
    keep = (lane < HEAD_DIM) if kvh == 0 else (lane >= HEAD_DIM)
    return jnp.concatenate(
        [jnp.where(keep, q[:, g * LANES:(g + 1) * LANES], jnp.zeros((), q.dtype)) for g in range(GQA_GROUP)],
        axis=0)


def _group_sum(p, nq):
    return p[0:nq] + p[nq:2 * nq] + p[2 * nq:3 * nq] + p[3 * nq:4 * nq]


def _importance(psum, ov):
    hi = psum.astype(BF16)
    lo = (psum - hi.astype(F32)).astype(BF16)
    return _dot(hi, ov) + _dot(lo, ov)


def _select_blocks(imp, qpos, n_blocks):
    w = imp.shape[1]
    blk = lax.broadcasted_iota(jnp.int32, (1, w), 1)
    cur = qpos >> 6
    forced = (blk == 0) | (blk == cur) | (blk == cur - 1)
    causal = blk * SEL_BLOCK <= qpos
    score = jnp.where(causal, imp + SEL_BONUS * forced.astype(F32), NEG_INF)
    score = jnp.where(blk < n_blocks, score, PAD_SCORE)
    rank = jnp.zeros(score.shape, F32)
    for j in range(n_blocks):
        col = score[:, j:j + 1]
        beats = (col > score) | ((col == score) & (blk > j))
        rank = rank + jnp.where(beats, 1.0, 0.0)
    return jnp.where(rank < float(min(N_SEL, n_blocks)), 1.0, 0.0)


def _gate_rows(gates, kvh, branch):
    cols = [(kvh * GQA_GROUP + g) * N_BRANCH + branch for g in range(GQA_GROUP)]
    return jnp.concatenate([gates[:, c:c + 1] for c in cols], axis=0)


def _online_update(state, s, mask, v):
    m, l, acc = state
    sm = jnp.where(mask, s, NEG_INF)
    m_new = jnp.maximum(m, jnp.max(sm, axis=-1, keepdims=True))
    alpha = jnp.exp(m - m_new)
    e = jnp.where(mask, jnp.exp(sm - m_new), 0.0)
    l = alpha * l + jnp.sum(e, axis=-1, keepdims=True)
    acc = alpha * acc + _dot(e.astype(BF16), v)
    return m_new, l, acc


def _online_init(rows):
    return (jnp.full((rows, 1), NEG_INF, F32), jnp.zeros((rows, 1), F32), jnp.zeros((rows, KV_DIM), F32))


def _online_finish(state):
    _, l, acc = state
    return acc / jnp.maximum(l, 1e-30)


def _merge_heads(o_ref, outs, nq):
    lane = lax.broadcasted_iota(jnp.int32, (1, LANES), 1)
    for g in range(GQA_GROUP):
        o_ref[:, g * LANES:(g + 1) * LANES] = jnp.where(lane < HEAD_DIM, outs[0][g * nq:(g + 1) * nq],
                                                         outs[1][g * nq:(g + 1) * nq])


def _prompt_attn_kernel(q_ref, gate_ref, kc_ref, vc_ref, ks_ref, vs_ref, kw_ref, vw_ref, ov_ref,
                        eexp_ref, o_ref, *, TQ, n_cmp, n_blocks):
    i = pl.program_id(1)
    TK = TQ
    R = GQA_GROUP * TQ
    s0 = i * TQ
    qpos = s0 + (lax.broadcasted_iota(jnp.int32, (R, 1), 0) & (TQ - 1))
    qpos_t = s0 + lax.broadcasted_iota(jnp.int32, (TQ, 1), 0)
    kiota = lax.broadcasted_iota(jnp.int32, (1, TK), 1)
    n_idx = lax.broadcasted_iota(jnp.int32, (1, kc_ref.shape[0]), 1)
    q = q_ref[...]
    gates = gate_ref[...]
    outs = []
    for kvh in range(N_KV_HEADS):
        qz = _query_rows(q, kvh)
        cmask = (n_idx < n_cmp) & (n_idx * CMP_STRIDE + (CMP_LEN - 1) <= qpos)
        p = _masked_softmax(_dot_t(qz, kc_ref[...]), cmask)
        o_cmp = _dot(p.astype(BF16), vc_ref[...])
        sel = _select_blocks(_importance(_group_sum(p, TQ), ov_ref[...]), qpos_t, n_blocks).astype(BF16)

        def slc_body(kt, state):
            start = pl.multiple_of(kt * TK, TK)
            k = ks_ref[pl.ds(start, TK), :]
            v = vs_ref[pl.ds(start, TK), :]
            mq = _dot(sel, eexp_ref[kt]) > 0.5
            mask = jnp.concatenate([mq] * GQA_GROUP, axis=0) & (kt * TK + kiota <= qpos)
            return _online_update(state, _dot_t(qz, k), mask, v)

        o_slc = _online_finish(lax.fori_loop(0, i + 1, slc_body, _online_init(R)))

        state = _online_init(R)
        for j in range(WINDOW // TK + 1):
            tw = i - WINDOW // TK + j
            start = pl.multiple_of(jnp.maximum(tw, 0) * TK, TK)
            kpos = tw * TK + kiota
            dist = qpos - kpos
            mask = (dist >= 0) & (dist < WINDOW) & (kpos >= 0)
            state = _online_update(state, _dot_t(qz, kw_ref[pl.ds(start, TK), :]), mask,
                                   vw_ref[pl.ds(start, TK), :])
        o_win = _online_finish(state)
        outs.append(o_cmp * _gate_rows(gates, kvh, 0) + o_slc * _gate_rows(gates, kvh, 1)
                    + o_win * _gate_rows(gates, kvh, 2))
    _merge_heads(o_ref, outs, TQ)


def _prompt_attn_call(q, gates, kc, vc, ksb, vsb, kwb, vwb, n_batch):
    T = q.shape[0] // n_batch
    TQ = 256
    n_qt = T // TQ
    n_ch = T // CMP_STRIDE
    n_cmp = n_ch - CMP_LEN // CMP_STRIDE + 1
    n_blocks = -(-T // SEL_BLOCK)
    assert n_ch <= LANES and n_blocks <= LANES and T % TQ == 0 and WINDOW % TQ == 0
    n = np.arange(n_ch)[:, None]
    j = np.arange(LANES)[None, :]
    ov = ((n * CMP_STRIDE < (j + 1) * SEL_BLOCK) & (n * CMP_STRIDE + CMP_LEN > j * SEL_BLOCK)
          & (n < n_cmp) & (j < n_blocks))
    ov = jnp.asarray(ov.astype(np.float32), BF16)
    kk = np.arange(T)
    eexp = (kk[None, :] // SEL_BLOCK == np.arange(LANES)[:, None]).astype(np.float32)
    eexp = jnp.asarray(eexp.reshape(LANES, n_qt, TQ).transpose(1, 0, 2), BF16)
    qtile = lambda w: pl.BlockSpec((TQ, w), lambda b, i: (b * n_qt + i, 0))
    perb = lambda r: pl.BlockSpec((r, KV_DIM), lambda b, i: (b, 0))
    return pl.pallas_call(
        functools.partial(_prompt_attn_kernel, TQ=TQ, n_cmp=n_cmp, n_blocks=n_blocks),
        grid=(n_batch, n_qt),
        in_specs=[qtile(D_ATTN), qtile(LANES), perb(n_ch), perb(n_ch), perb(T), perb(T), perb(T), perb(T),
                  pl.BlockSpec((n_ch, LANES), lambda b, i: (0, 0)),
                  pl.BlockSpec((n_qt, LANES, TQ), lambda b, i: (0, 0, 0))],
        out_specs=qtile(D_ATTN),
        out_shape=jax.ShapeDtypeStruct((n_batch * T, D_ATTN), F32),
        compiler_params=pltpu.CompilerParams(dimension_semantics=("arbitrary", "arbitrary"),
                                             vmem_limit_bytes=VMEM_LIMIT),
        name="attn_prompt",
    )(q, gates, kc, vc, ksb, vsb, kwb, vwb, ov, eexp)


def _sample_attn_kernel(pt_ref, q_ref, gate_ref, kc_ref, vc_ref, kpool, vpool, ksn_ref, vsn_ref,
                        kws_ref, vws_ref, kwn_ref, vwn_ref, ov_ref, eexp_ref, o_ref, kwo_ref, vwo_ref,
                        kbuf, vbuf, sem, *, n_pages, page, n_cmp, n_blocks, past, ts):
    R = GQA_GROUP * ts
    n_past_blocks = past // SEL_BLOCK
    wbuf = kws_ref.shape[2]
    slot = _page_gather(pt_ref, (kpool, vpool), (kbuf, vbuf), sem,
                        lambda buf, s, p: buf.at[s, :, pl.ds(pl.multiple_of(p * page, page), page)], n_pages)

    q = q_ref[0]
    gates = gate_ref[0]
    t_row = lax.broadcasted_iota(jnp.int32, (R, 1), 0) & (ts - 1)
    qpos = past + t_row
    qpos_t = past + lax.broadcasted_iota(jnp.int32, (ts, 1), 0)
    lane = lax.broadcasted_iota(jnp.int32, (1, LANES), 1)
    n_idx = lax.broadcasted_iota(jnp.int32, (1, kc_ref.shape[1]), 1)
    pad_rows = jnp.zeros((LANES - ts, KV_DIM), F32)
    new_tile = lambda ref: jnp.concatenate([ref[0], pad_rows], axis=0).astype(BF16)
    kpos_new = past + lane
    dist_new = qpos - kpos_new
    new_valid = (lane < ts) & (dist_new >= 0)

    def two_part_attention(qz, k_t, v_t, mask_old, k_new, v_new, mask_new):
        so = jnp.where(mask_old, _dot(qz, k_t), NEG_INF)
        sn = jnp.where(mask_new, _dot_t(qz, k_new), NEG_INF)
        m = jnp.maximum(jnp.max(so, axis=-1, keepdims=True), jnp.max(sn, axis=-1, keepdims=True))
        eo = jnp.where(mask_old, jnp.exp(so - m), 0.0)
        en = jnp.where(mask_new, jnp.exp(sn - m), 0.0)
        l = jnp.sum(eo, axis=-1, keepdims=True) + jnp.sum(en, axis=-1, keepdims=True)
        return (_dot_t(eo.astype(BF16), v_t) + _dot(en.astype(BF16), v_new)) / jnp.maximum(l, 1e-30)

    kw_state = kws_ref[0]
    vw_state = vws_ref[0]
    kw_state_b = kw_state.astype(BF16)
    vw_state_b = vw_state.astype(BF16)
    k_past = kbuf[slot].astype(BF16)
    v_past = vbuf[slot].astype(BF16)
    wpos = past - wbuf + lax.broadcasted_iota(jnp.int32, (1, wbuf), 1)
    wdist = qpos - wpos
    wmask = (wdist >= 0) & (wdist < WINDOW) & (wpos >= 0)
    wmask_new = new_valid & (dist_new < WINDOW)
    outs = []
    for kvh in range(N_KV_HEADS):
        qz = _query_rows(q, kvh)
        cmask = (n_idx < n_cmp) & (n_idx * CMP_STRIDE + (CMP_LEN - 1) <= qpos)
        p = _masked_softmax(_dot_t(qz, kc_ref[0]), cmask)
        o_cmp = _dot(p.astype(BF16), vc_ref[0])
        sel = _select_blocks(_importance(_group_sum(p, ts), ov_ref[...]), qpos_t, n_blocks)
        o_win = two_part_attention(qz, kw_state_b, vw_state_b, wmask, new_tile(kwn_ref), new_tile(vwn_ref),
                                   wmask_new)
        mq = _dot(sel[:, 0:n_past_blocks].astype(BF16), eexp_ref[...]) > 0.5
        pmask = jnp.concatenate([mq] * GQA_GROUP, axis=0)
        sel_new = jnp.concatenate([sel[:, n_past_blocks:n_past_blocks + 1]] * GQA_GROUP, axis=0) > 0.5
        o_slc = two_part_attention(qz, k_past, v_past, pmask, new_tile(ksn_ref), new_tile(vsn_ref),
                                   new_valid & sel_new)
        outs.append(o_cmp * _gate_rows(gates, kvh, 0) + o_slc * _gate_rows(gates, kvh, 1)
                    + o_win * _gate_rows(gates, kvh, 2))
    for g in range(GQA_GROUP):
        o_ref[0, :, g * LANES:(g + 1) * LANES] = jnp.where(
            lane < HEAD_DIM, outs[0][g * ts:(g + 1) * ts], outs[1][g * ts:(g + 1) * ts])

    def shifted(state, new_ref, out_ref):
        rolled = pltpu.roll(state, wbuf - ts, 1)
        new_t = pltpu.roll(jnp.concatenate([new_ref[0], pad_rows], axis=0).T, LANES - ts, 1)
        out_ref[0, :, 0:wbuf - LANES] = rolled[:, 0:wbuf - LANES]
        out_ref[0, :, wbuf - LANES:wbuf] = jnp.where(lane >= LANES - ts, new_t, rolled[:, wbuf - LANES:wbuf])

    shifted(kw_state, kwn_ref, kwo_ref)
    shifted(vw_state, vwn_ref, vwo_ref)


def _sample_attn_call(page_table, q3, gates3, kc, vc, kpool_t, vpool_t, ks_new, vs_new, kw_state_t, vw_state_t,
                      kw_new, vw_new):
    n_batch, n_pages = page_table.shape
    page = kpool_t.shape[2]
    ts = q3.shape[1]
    past = n_pages * page
    n_ch = kc.shape[1]
    n_cmp = n_ch - CMP_LEN // CMP_STRIDE + 1
    n_blocks = -(-(past + ts) // SEL_BLOCK)
    wbuf = kw_state_t.shape[2]
    assert past % SEL_BLOCK == 0 and ts <= SEL_BLOCK and past // SEL_BLOCK == LANES and wbuf > LANES
    w_sel = 2 * LANES
    n = np.arange(n_ch)[:, None]
    j = np.arange(w_sel)[None, :]
    ov = ((n * CMP_STRIDE < (j + 1) * SEL_BLOCK) & (n * CMP_STRIDE + CMP_LEN > j * SEL_BLOCK)
          & (n < n_cmp) & (j < n_blocks))
    ov = jnp.asarray(ov.astype(np.float32), BF16)
    eexp = jnp.asarray((np.arange(past)[None, :] // SEL_BLOCK == np.arange(LANES)[:, None]).astype(np.float32),
                       BF16)
    perb = lambda a: pl.BlockSpec((1,) + a.shape[1:], lambda b, pt: (b, 0, 0))
    const2 = lambda a: pl.BlockSpec(a.shape, lambda b, pt: (0, 0))
    anyspec = pl.BlockSpec(memory_space=pl.ANY)
    grid_spec = pltpu.PrefetchScalarGridSpec(
        num_scalar_prefetch=1, grid=(n_batch,),
        in_specs=[perb(q3), perb(gates3), perb(kc), perb(vc), anyspec, anyspec, perb(ks_new), perb(vs_new),
                  perb(kw_state_t), perb(vw_state_t), perb(kw_new), perb(vw_new), const2(ov), const2(eexp)],
        out_specs=[pl.BlockSpec((1, ts, D_ATTN), lambda b, pt: (b, 0, 0)), perb(kw_state_t), perb(vw_state_t)],
        scratch_shapes=[pltpu.VMEM((2, KV_DIM, past), F32), pltpu.VMEM((2, KV_DIM, past), F32),
                        pltpu.SemaphoreType.DMA((2, 2))])
    state_sd = jax.ShapeDtypeStruct(kw_state_t.shape, F32)
    return pl.pallas_call(
        functools.partial(_sample_attn_kernel, n_pages=n_pages, page=page, n_cmp=n_cmp, n_blocks=n_blocks,
                          past=past, ts=ts),
        grid_spec=grid_spec,
        out_shape=[jax.ShapeDtypeStruct((n_batch, ts, D_ATTN), F32), state_sd, state_sd],
        compiler_params=pltpu.CompilerParams(dimension_semantics=("arbitrary",),
                                             vmem_limit_bytes=VMEM_LIMIT),
        name="attn_sample",
    )(page_table, q3, gates3, kc, vc, kpool_t, vpool_t, ks_new, vs_new, kw_state_t, vw_state_t, kw_new, vw_new,
      ov, eexp)


def _outffn_kernel(x_ref, o_ref, v_ref, g1_ref, sh2_ref, sc2_ref, g2_ref, ag_ref, ln2_ref, woa_ref, wov_ref,
                   wgu_ref, wd_ref, y_ref, *, G, L, ff_chunk):
    M = G * L

    def expand(ref):
        v = ref[...]
        return jnp.broadcast_to(v, (G, L, v.shape[-1])).reshape(M, v.shape[-1])

    a = (_rms(o_ref[...]) * ag_ref[...]).astype(BF16)
    mix = _dot(a, woa_ref[...]) + _dot(v_ref[...], wov_ref[...])
    x1 = x_ref[...] + expand(g1_ref) * mix
    h2 = (_rms(x1) * ln2_ref[...] * (1.0 + expand(sc2_ref)) + expand(sh2_ref)).astype(BF16)
    y = jnp.zeros((M, D_MODEL), F32)
    for c in range(D_FF // ff_chunk):
        lo = c * ff_chunk
        g = _dot(h2, wgu_ref[:, lo:lo + ff_chunk])
        u = _dot(h2, wgu_ref[:, D_FF + lo:D_FF + lo + ff_chunk])
        y = y + _dot((_silu(g) * u).astype(BF16), wd_ref[lo:lo + ff_chunk, :])
    y_ref[...] = x1 + expand(g2_ref) * y


def _outffn_call(x2, o2, v2, mod3, mod_off, G, L, tiles_per_group, wts, name):
    ntok = x2.shape[0]
    M = G * L
    n_tiles = ntok // M
    ag, ln2, woa, wov, wgu, wd = wts
    mod_idx = (lambda i: i) if G > 1 else (lambda i: mod_off + i // tiles_per_group)
    tok = lambda n: pl.BlockSpec((M, n), lambda i: (i, 0))
    modspec = lambda col: pl.BlockSpec((G, 1, D_MODEL), lambda i: (mod_idx(i), 0, col))
    const2 = lambda a: pl.BlockSpec(a.shape, lambda i: (0, 0))
    return pl.pallas_call(
        functools.partial(_outffn_kernel, G=G, L=L, ff_chunk=D_FF // 2),
        grid=(n_tiles,),
        in_specs=[tok(D_MODEL), tok(D_ATTN), tok(D_CONV), modspec(2), modspec(3), modspec(4), modspec(5),
                  const2(ag), const2(ln2), const2(woa), const2(wov), const2(wgu), const2(wd)],
        out_specs=tok(D_MODEL),
        out_shape=jax.ShapeDtypeStruct((ntok, D_MODEL), F32),
        compiler_params=pltpu.CompilerParams(dimension_semantics=("arbitrary",),
                                             vmem_limit_bytes=VMEM_LIMIT),
        name=name,
    )(x2, o2, v2, mod3, mod3, mod3, mod3, ag, ln2, woa, wov, wgu, wd)


def _rope_tables(pos):
    half = ROT_DIM // 2
    n = pos.shape[0]
    inv = ROPE_THETA ** (-jnp.arange(half, dtype=F32) * 2.0 / ROT_DIM)
    ang = pos.astype(F32)[:, None] * inv[None, :]
    cos, sin = jnp.cos(ang), jnp.sin(ang)
    rest = HEAD_DIM - ROT_DIM
    z8 = jnp.zeros((n, half), F32)
    c64 = jnp.concatenate([cos, cos, jnp.ones((n, rest), F32)], axis=1)
    a64 = jnp.concatenate([z8, sin, jnp.zeros((n, rest), F32)], axis=1)
    b64 = jnp.concatenate([-sin, z8, jnp.zeros((n, rest), F32)], axis=1)
    two = lambda a: jnp.concatenate([a, a], axis=1)
    return two(c64), two(a64), two(b64)


def _head_perm_cols(w):
    lead = w.shape[:-1]
    return w.reshape(lead + (N_KV_HEADS, GQA_GROUP, HEAD_DIM)).swapaxes(-3, -2).reshape(lead + (D_ATTN,))


def _compress_weights(w1, w2):
    r_n = CMP_LEN // CMP_STRIDE
    w1r = w1.reshape(r_n, CMP_STRIDE, HEAD_DIM, HEAD_DIM)
    eye = jnp.eye(N_KV_HEADS, dtype=F32)
    w1big = jnp.einsum('rpdh,kl->pkdrlh', w1r, eye).reshape(CMP_STRIDE * KV_DIM, r_n * KV_DIM)
    w2big = jnp.einsum('dh,kl->kdlh', w2, eye).reshape(KV_DIM, KV_DIM)
    return w1big.astype(BF16), w2big.astype(BF16)


def kernel(x_prompt, x_sample, cache_k_cmp, cache_v_cmp, cache_k_slc, cache_v_slc, state_k_win, state_v_win,
           state_conv, page_table, c_prompt, c_sample, w_ada, b_ada, ln1_g, w_in, q_norm_g, k_norm_g, pe_cmp,
           w_ck1, w_ck2, w_cv1, w_cv2, w_conv, attn_out_g, conv_out_g, w_o, ln2_g, w_gu, w_down):
    depth = w_in.shape[0]
    bp, tp, _ = x_prompt.shape
    bs, ts, _ = x_sample.shape
    past = page_table.shape[1] * cache_k_cmp.shape[2]
    wbuf = state_k_win.shape[2]
    assert wbuf == WINDOW and tp >= WINDOW and ts == 8 and bs % 64 == 0

    tab_p = tuple(a[None] for a in _rope_tables(jnp.arange(tp)))
    tab_s = tuple(a[None] for a in _rope_tables(past + jnp.arange(ts)))
    end_p = jnp.arange(tp // CMP_STRIDE) * CMP_STRIDE + CMP_LEN - 1
    end_s = jnp.arange(past // CMP_STRIDE) * CMP_STRIDE + CMP_LEN - 1
    tab_cp = _rope_tables(end_p)
    tab_cs = _rope_tables(end_s)
    p128 = jnp.asarray(np.kron(np.eye(2), np.full((HEAD_DIM, HEAD_DIM), 1.0 / HEAD_DIM)), BF16)

    xp = x_prompt.reshape(bp * tp, D_MODEL)
    xs = x_sample.reshape(bs * ts, D_MODEL)
    c_all = jnp.concatenate([c_sample, c_prompt], axis=0)
    heads5 = lambda a, b, t: a.reshape(b, t, N_KV_HEADS, HEAD_DIM)
    to_t = lambda a: a.transpose(0, 2, 3, 1).reshape(a.shape[0], KV_DIM, a.shape[1])
    from_t = lambda a: a.reshape(a.shape[0], N_KV_HEADS, HEAD_DIM, a.shape[2]).transpose(0, 3, 1, 2)
    per_layer = []
    for l in range(depth):
        wl = w_in[l]
        o_kv = D_ATTN
        o_g = o_kv + 6 * KV_DIM
        o_c = o_g + N_BRANCH * N_HEADS
        w_perm = jnp.concatenate(
            [_head_perm_cols(wl[:, :D_ATTN]), wl[:, o_kv:o_g], wl[:, o_c:o_c + 3 * D_CONV], wl[:, o_g:o_c],
             jnp.zeros((D_MODEL, LANES - N_BRANCH * N_HEADS), F32)], axis=1).astype(BF16)
        qg = jnp.tile(q_norm_g[l], N_HEADS)[None]
        kg = [jnp.tile(k_norm_g[l, i], N_KV_HEADS)[None] for i in range(N_BRANCH)]
        in_wts = (w_perm, qg, kg[1], kg[2], p128, w_conv[l], conv_out_g[l][None], ln1_g[l][None])
        pe_rows = jnp.broadcast_to(pe_cmp[l].reshape(CMP_LEN // CMP_STRIDE, CMP_STRIDE, 1, HEAD_DIM),
                                   (CMP_LEN // CMP_STRIDE, CMP_STRIDE, N_KV_HEADS, HEAD_DIM))
        pe_rows = jnp.concatenate([pe_rows.reshape(CMP_LEN // CMP_STRIDE, CMP_STRIDE * KV_DIM),
                                   jnp.zeros((8 - CMP_LEN // CMP_STRIDE, CMP_STRIDE * KV_DIM), F32)],
                                  axis=0).astype(BF16)
        cw = _compress_weights(w_ck1[l], w_ck2[l]) + _compress_weights(w_cv1[l], w_cv2[l]) + (pe_rows, kg[0], p128)
        out_wts = (_head_perm_cols(attn_out_g[l])[None], ln2_g[l][None],
                   _head_perm_cols(w_o[l][:D_ATTN].T).T.astype(BF16), w_o[l][D_ATTN:].astype(BF16),
                   w_gu[l].astype(BF16), w_down[l].astype(BF16))

        mod3 = _adaln_call(c_all, w_ada[l], b_ada[l]).reshape(bs + bp, 1, 6 * D_MODEL)

        tm = 512
        (q, kc_t, vc_t, ks_t, vs_t, kw_t, vw_t, kcb, vcb, ksb, vsb, kwb, vwb, gates, vconv, p_conv_l) = _inproj_call(
            xp, mod3, bs, 1, tm, tp // tm, tab_p, None, in_wts)
        kc, vc = _prompt_compress_call(kcb, vcb, bp, cw, tab_cp)
        o = _prompt_attn_call(q, gates, kc, vc, ksb, vsb, kwb, vwb, bp)
        xp = _outffn_call(xp, o, vconv, mod3, bs, 1, tm, tp // tm, out_wts, "outffn_prompt")
        prompt_state = tuple(from_t(a) for a in (kc_t, vc_t, ks_t, vs_t, kw_t[:, :, tp - WINDOW:],
                                                 vw_t[:, :, tp - WINDOW:])) + (p_conv_l,)

        gs = 64
        (q, kc_r, vc_r, ks, vs, kw, vw, _, _, _, _, _, _, gates, vconv, s_conv_l) = _inproj_call(
            xs, mod3, 0, gs, ts, 1, tab_s, state_conv[l], in_wts)
        kc, vc = _sample_compress_call(page_table, to_t(cache_k_cmp[l]), to_t(cache_v_cmp[l]), cw, tab_cs)
        r3 = lambda a: a.reshape(bs, ts, a.shape[-1])
        o, k_win_t, v_win_t = _sample_attn_call(
            page_table, r3(q), r3(gates), kc, vc, to_t(cache_k_slc[l]), to_t(cache_v_slc[l]), r3(ks), r3(vs),
            to_t(state_k_win[l]), to_t(state_v_win[l]), r3(kw), r3(vw))
        xs = _outffn_call(xs, o.reshape(bs * ts, D_ATTN), vconv, mod3, 0, gs, ts, 1, out_wts, "outffn_sample")
        sample_state = (heads5(kc_r, bs, ts), heads5(vc_r, bs, ts), heads5(ks, bs, ts), heads5(vs, bs, ts),
                        from_t(k_win_t), from_t(v_win_t), s_conv_l)
        per_layer.append(prompt_state + sample_state)

    states = [jnp.stack(z) for z in zip(*per_layer)]
    return (xp.reshape(bp, tp, D_MODEL), xs.reshape(bs, ts, D_MODEL)) + tuple(states)
```

```python
import functools

import numpy as np
import jax
import jax.numpy as jnp
from jax import lax
from jax.experimental import pallas as pl
from jax.experimental.pallas import tpu as pltpu

F32 = jnp.float32
BF16 = jnp.bfloat16

D_MODEL = 1024
HEAD_DIM = 64
N_HEADS = 8
N_KV_HEADS = 2
GQA_GROUP = 4
KV_DIM = N_KV_HEADS * HEAD_DIM
D_ATTN = N_HEADS * HEAD_DIM
D_CONV = D_MODEL - D_ATTN
N_BRANCH = 3
CMP_LEN = 32
CMP_STRIDE = 16
SEL_BLOCK = 64
N_SEL = 16
WINDOW = 512
ROT_DIM = HEAD_DIM // 4
ROPE_THETA = 500000.0
CONV_W = 3
D_FF = 2816
EPS = 1e-6
NEG_INF = -1e30
SEL_BONUS = 1e3
PAD_SCORE = -3e38
MASK_BIAS = -(2.0 ** 100)
LOG2E = 1.4426950408889634

LANES = 128
N_IN_PAD = D_ATTN + 6 * KV_DIM + 3 * D_CONV + LANES
VMEM_LIMIT = 56 * 1024 * 1024
PAGES_PER_ITER = 8


def _dot(a, b):
    return jnp.dot(a, b, preferred_element_type=F32)


def _dot_t(a, b):
    return lax.dot_general(a, b, (((1,), (1,)), ((), ())), preferred_element_type=F32)


def _rms(x):
    return x * lax.rsqrt(jnp.mean(x * x, axis=-1, keepdims=True) + EPS)


def _silu(x):
    return x * jax.nn.sigmoid(x)


def _head_rms(z, p128):
    ms = _dot((z * z).astype(BF16), p128)
    return z * lax.rsqrt(ms + EPS)


def _rope(z, cos, sa, sb):
    return z * cos + pltpu.roll(z, 8, 1) * sa + pltpu.roll(z, LANES - 8, 1) * sb


def _masked_softmax2(s, mask):
    sm = jnp.where(mask, s, NEG_INF)
    m = jnp.max(sm, axis=-1, keepdims=True)
    e = jnp.where(mask, jnp.exp2(sm - m), 0.0)
    return e / jnp.maximum(jnp.sum(e, axis=-1, keepdims=True), 1e-30)


def _adaln_kernel(c_ref, w_ref, b_ref, o_ref):
    a = _silu(c_ref[...]).astype(BF16)
    o_ref[...] = _dot(a, w_ref[...].astype(BF16)) + b_ref[...]


def _adaln_call(c_all, w_ada, b_ada):
    n, d = c_all.shape
    n_out = w_ada.shape[1]
    bn = 512
    return pl.pallas_call(
        _adaln_kernel,
        grid=(n_out // bn,),
        in_specs=[pl.BlockSpec((n, d), lambda j: (0, 0)),
                  pl.BlockSpec((d, bn), lambda j: (0, j)),
                  pl.BlockSpec((1, bn), lambda j: (0, j))],
        out_specs=pl.BlockSpec((n, bn), lambda j: (0, j)),
        out_shape=jax.ShapeDtypeStruct((n, n_out), F32),
        compiler_params=pltpu.CompilerParams(dimension_semantics=("arbitrary",),
                                             vmem_limit_bytes=VMEM_LIMIT),
        name="adaln",
    )(c_all, w_ada, b_ada.reshape(1, n_out))


def _inproj_kernel(*refs, G, L, tiles_per_group, has_prev):
    (x_ref, shift_ref, scale_ref, ln1_ref, w_ref, qg_ref, ksg_ref, kwg_ref, p128_ref,
     cos_ref, sa_ref, sb_ref, wconv_ref, convg_ref) = refs[:14]
    pos = 14
    prev_ref = None
    if has_prev:
        prev_ref = refs[pos]
        pos += 1
    q_o = refs[pos]
    state_o = refs[pos + 1:pos + 7]
    copy_o = refs[pos + 7:pos + 13]
    gate_o, vconv_o, cstate_o = refs[pos + 13:pos + 16]
    carry_ref = None if has_prev else refs[pos + 16]
    native_states = not has_prev
    M = G * L

    def expand(v):
        return jnp.broadcast_to(v, (G, L, v.shape[-1])).reshape(M, v.shape[-1])

    x = x_ref[...]
    xn = _rms(x) * ln1_ref[...]
    h = (xn * (1.0 + expand(scale_ref[...])) + expand(shift_ref[...])).astype(BF16)

    p128 = p128_ref[...]
    cos = expand(cos_ref[...])
    sa = expand(sa_ref[...])
    sb = expand(sb_ref[...])

    zq = _dot(h, w_ref[:, 0:D_ATTN])
    for c in range(D_ATTN // LANES):
        sl = slice(c * LANES, (c + 1) * LANES)
        z = _head_rms(zq[:, sl], p128) * qg_ref[:, sl]
        q_o[:, sl] = (_rope(z, cos, sa, sb) * (HEAD_DIM ** -0.5 * LOG2E)).astype(BF16)

    zkv = _dot(h, w_ref[:, D_ATTN:D_ATTN + 6 * KV_DIM])
    ks = _rope(_head_rms(zkv[:, 256:384], p128) * ksg_ref[...], cos, sa, sb)
    kw = _rope(_head_rms(zkv[:, 512:640], p128) * kwg_ref[...], cos, sa, sb)
    rows = (zkv[:, 0:128], zkv[:, 128:256], ks, zkv[:, 384:512], kw, zkv[:, 640:768])
    for r, st_o, b_o in zip(rows, state_o, copy_o):
        if native_states:
            st_o[0] = r.T
        else:
            st_o[...] = r
        b_o[...] = r.astype(BF16)

    c0 = D_ATTN + 6 * KV_DIM
    gate_o[...] = jax.nn.sigmoid(_dot(h, w_ref[:, c0 + 3 * D_CONV:c0 + 3 * D_CONV + LANES]))

    zc = _dot(h, w_ref[:, c0:c0 + 3 * D_CONV])
    hc = zc[:, 0:D_CONV]
    bg = zc[:, D_CONV:2 * D_CONV]
    cg = zc[:, 2 * D_CONV:3 * D_CONV]
    u = cg * hc
    if has_prev:
        prev = prev_ref[...]
    else:
        first = (pl.program_id(0) % tiles_per_group) == 0
        prev = jnp.where(first, 0.0, carry_ref[6:8, :])[None]
    prev_a = expand(prev[:, 0:1, :])
    prev_b = expand(prev[:, 1:2, :])
    t = lax.broadcasted_iota(jnp.int32, (M, 1), 0) & (L - 1)
    u1 = jnp.where(t == 0, prev_b, pltpu.roll(u, 1, 0))
    u2 = jnp.where(t == 0, prev_a, jnp.where(t == 1, prev_b, pltpu.roll(u, 2, 0)))
    cy = u2 * wconv_ref[0:1, :] + u1 * wconv_ref[1:2, :] + u * wconv_ref[2:3, :]
    vconv_o[...] = (_rms(bg * cy) * convg_ref[...]).astype(BF16)
    cstate_o[...] = u.reshape(G, L, D_CONV)[:, L - 2:L, :]
    if not has_prev:
        carry_ref[...] = u[M - 8:M, :]


def _inproj_call(x2, mod3, mod_off, G, L, tiles_per_group, tabs, prev3, wts):
    ntok = x2.shape[0]
    M = G * L
    n_tiles = ntok // M
    has_prev = prev3 is not None
    if has_prev:
        assert tiles_per_group == 1
        n_groups = ntok // L
        mod_idx = lambda i: i
        tab_idx = lambda i: 0
    else:
        assert G == 1
        n_groups = n_tiles // tiles_per_group
        mod_idx = lambda i: mod_off + i // tiles_per_group
        tab_idx = lambda i: i % tiles_per_group
    assert L & (L - 1) == 0
    cos, sa, sb = tabs
    w_in, qg, ksg, kwg, p128, wconv, convg, ln1 = wts

    const2 = lambda shape: pl.BlockSpec(shape, lambda i: (0, 0))
    tok = lambda n: pl.BlockSpec((M, n), lambda i: (i, 0))
    tabspec = pl.BlockSpec((1, L, LANES), lambda i: (0, tab_idx(i), 0))
    in_specs = [tok(D_MODEL),
                pl.BlockSpec((G, 1, D_MODEL), lambda i: (mod_idx(i), 0, 0)),
                pl.BlockSpec((G, 1, D_MODEL), lambda i: (mod_idx(i), 0, 1)),
                const2((1, D_MODEL)), const2((D_MODEL, N_IN_PAD)), const2((1, D_ATTN)),
                const2((1, KV_DIM)), const2((1, KV_DIM)), const2((LANES, LANES)),
                tabspec, tabspec, tabspec, const2((CONV_W, D_CONV)), const2((1, D_CONV))]
    args = [x2, mod3, mod3, ln1, w_in, qg, ksg, kwg, p128, cos, sa, sb, wconv, convg]
    if has_prev:
        in_specs.append(pl.BlockSpec((G, CONV_W - 1, D_CONV), lambda i: (i, 0, 0)))
        args.append(prev3)
    f32tok = lambda n: jax.ShapeDtypeStruct((ntok, n), F32)
    b16tok = lambda n: jax.ShapeDtypeStruct((ntok, n), BF16)
    if has_prev:
        state_sd, state_spec = f32tok(KV_DIM), tok(KV_DIM)
        cstate_spec = pl.BlockSpec((G, CONV_W - 1, D_CONV), lambda i: (i, 0, 0))
    else:
        state_sd = jax.ShapeDtypeStruct((n_groups, KV_DIM, tiles_per_group * L), F32)
        state_spec = pl.BlockSpec((1, KV_DIM, L), lambda i: (i // tiles_per_group, 0, i % tiles_per_group))
        cstate_spec = pl.BlockSpec((1, CONV_W - 1, D_CONV), lambda i: (i // tiles_per_group, 0, 0))
    out_shape = ([b16tok(D_ATTN)] + [state_sd] * 6 + [b16tok(KV_DIM)] * 6
                 + [f32tok(LANES), b16tok(D_CONV),
                    jax.ShapeDtypeStruct((n_groups, CONV_W - 1, D_CONV), F32)])
    out_specs = ([tok(D_ATTN)] + [state_spec] * 6 + [tok(KV_DIM)] * 6 + [tok(LANES), tok(D_CONV), cstate_spec])
    scratch = [] if has_prev else [pltpu.VMEM((8, D_CONV), F32)]
    return pl.pallas_call(
        functools.partial(_inproj_kernel, G=G, L=L, tiles_per_group=tiles_per_group,
                          has_prev=has_prev),
        grid=(n_tiles,),
        in_specs=in_specs, out_specs=out_specs, out_shape=out_shape,
        scratch_shapes=scratch,
        compiler_params=pltpu.CompilerParams(dimension_semantics=("arbitrary",),
                                             vmem_limit_bytes=VMEM_LIMIT),
        name="inproj_sample" if has_prev else "inproj_prompt",
    )(*args)


def _compress_math(chunks, w1, w2, pe_rows, n_ch):
    proj = _dot(chunks, w1)
    ppe = _dot(pe_rows, w1)
    bias = ppe[0:1, 0:KV_DIM] + ppe[1:2, KV_DIM:2 * KV_DIM]
    pre = proj[:, 0:KV_DIM] + pltpu.roll(proj[:, KV_DIM:2 * KV_DIM], n_ch - 1, 0) + bias
    return _dot(_silu(pre).astype(BF16), w2)


def _compress_finish_k(kc, kg_ref, p128_ref, cos_ref, sa_ref, sb_ref):
    return _rope(_head_rms(kc, p128_ref[...]) * kg_ref[...], cos_ref[...], sa_ref[...], sb_ref[...])


def _prompt_compress_kernel(kr_ref, vr_ref, w1k_ref, w2k_ref, w1v_ref, w2v_ref, pe_ref, kg_ref,
                            p128_ref, cos_ref, sa_ref, sb_ref, kc_o, vc_o, *, n_ch):
    pe = pe_ref[...]
    kc = _compress_math(kr_ref[...].astype(BF16), w1k_ref[...], w2k_ref[...], pe, n_ch)
    kc_o[...] = _compress_finish_k(kc, kg_ref, p128_ref, cos_ref, sa_ref, sb_ref).astype(BF16)
    vc_o[...] = _compress_math(vr_ref[...].astype(BF16), w1v_ref[...], w2v_ref[...], pe,
                               n_ch).astype(BF16)


def _prompt_compress_call(kc_r, vc_r, n_batch, cw, tabs):
    T = kc_r.shape[0] // n_batch
    n_ch = T // CMP_STRIDE
    feat = CMP_STRIDE * KV_DIM
    w1k, w2k, w1v, w2v, pe_rows, kg, p128 = cw
    cos, sa, sb = tabs
    const2 = lambda a: pl.BlockSpec(a.shape, lambda b: (0, 0))
    chunk_spec = pl.BlockSpec((n_ch, feat), lambda b: (b, 0))
    out_spec = pl.BlockSpec((n_ch, KV_DIM), lambda b: (b, 0))
    out_sd = jax.ShapeDtypeStruct((n_batch * n_ch, KV_DIM), BF16)
    return pl.pallas_call(
        functools.partial(_prompt_compress_kernel, n_ch=n_ch),
        grid=(n_batch,),
        in_specs=[chunk_spec, chunk_spec] + [const2(a) for a in
                                             (w1k, w2k, w1v, w2v, pe_rows, kg, p128, cos, sa, sb)],
        out_specs=[out_spec, out_spec], out_shape=[out_sd, out_sd],
        compiler_params=pltpu.CompilerParams(dimension_semantics=("arbitrary",),
                                             vmem_limit_bytes=VMEM_LIMIT),
        name="compress_prompt",
    )(kc_r.reshape(n_batch * n_ch, feat), vc_r.reshape(n_batch * n_ch, feat),
      w1k, w2k, w1v, w2v, pe_rows, kg, p128, cos, sa, sb)


def _page_gather(pt_ref, pools, bufs, sem, dst_of, n_pages):
    b = pl.program_id(0)
    nb = pl.num_programs(0)
    slot = b % 2

    def copy(i, pg, p, s):
        return pltpu.make_async_copy(pools[i].at[pg], dst_of(bufs[i], s, p), sem.at[i, s])

    def issue(bb, s):
        def body(p, carry):
            pg = pt_ref[bb, p]
            for i in range(len(pools)):
                copy(i, pg, p, s).start()
            return carry
        lax.fori_loop(0, n_pages, body, 0)

    @pl.when(b == 0)
    def _():
        issue(0, 0)

    @pl.when(b + 1 < nb)
    def _():
        issue(b + 1, 1 - slot)

    def wait_body(p, carry):
        for i in range(len(pools)):
            copy(i, 0, p, slot).wait()
        return carry
    lax.fori_loop(0, n_pages, wait_body, 0)
    return slot


def _sample_compress_kernel(pt_ref, kpool, vpool, w1k_ref, w2k_ref, w1v_ref, w2v_ref, pe_ref,
                            kg_ref, p128_ref, cos_ref, sa_ref, sb_ref, kc_o, vc_o,
                            kbuf, vbuf, rows, sem, *, n_pages, page):
    n_ch = n_pages * page // CMP_STRIDE
    slot = _page_gather(pt_ref, (kpool, vpool), (kbuf, vbuf), sem,
                        lambda buf, s, p: buf.at[s, p], n_pages)

    halves = CMP_STRIDE // 8
    half_rows = n_ch * 8
    chunks_per_page = page // CMP_STRIDE

    def chunk_rows(buf):
        def body(i, carry):
            for u in range(PAGES_PER_ITER):
                p = i * PAGES_PER_ITER + u
                t = buf[slot, p].T.reshape(chunks_per_page, halves, 8, KV_DIM)
                for h in range(halves):
                    start = pl.multiple_of(h * half_rows + p * (chunks_per_page * 8), chunks_per_page * 8)
                    rows[pl.ds(start, chunks_per_page * 8), :] = t[:, h].reshape(chunks_per_page * 8, KV_DIM)
            return carry
        lax.fori_loop(0, n_pages // PAGES_PER_ITER, body, 0)
        return jnp.concatenate([rows[pl.ds((j // 8) * half_rows + j % 8, n_ch, stride=8), :].astype(BF16)
                                for j in range(CMP_STRIDE)], axis=1)

    pe = pe_ref[...]
    kc = _compress_math(chunk_rows(kbuf), w1k_ref[...], w2k_ref[...], pe, n_ch)
    kc_o[0] = _compress_finish_k(kc, kg_ref, p128_ref, cos_ref, sa_ref, sb_ref).astype(BF16)
    vc_o[0] = _compress_math(chunk_rows(vbuf), w1v_ref[...], w2v_ref[...], pe, n_ch).astype(BF16)


def _sample_compress_call(page_table, kpool_t, vpool_t, cw, tabs):
    n_batch, n_pages = page_table.shape
    page = kpool_t.shape[2]
    n_ch = n_pages * page // CMP_STRIDE
    w1k, w2k, w1v, w2v, pe_rows, kg, p128 = cw
    cos, sa, sb = tabs
    const2 = lambda a: pl.BlockSpec(a.shape, lambda b, pt: (0, 0))
    anyspec = pl.BlockSpec(memory_space=pl.ANY)
    out_spec = pl.BlockSpec((1, n_ch, KV_DIM), lambda b, pt: (b, 0, 0))
    out_sd = jax.ShapeDtypeStruct((n_batch, n_ch, KV_DIM), BF16)
    grid_spec = pltpu.PrefetchScalarGridSpec(
        num_scalar_prefetch=1, grid=(n_batch,),
        in_specs=[anyspec, anyspec] + [const2(a) for a in
                                       (w1k, w2k, w1v, w2v, pe_rows, kg, p128, cos, sa, sb)],
        out_specs=[out_spec, out_spec],
        scratch_shapes=[pltpu.VMEM((2, n_pages, KV_DIM, page), F32), pltpu.VMEM((2, n_pages, KV_DIM, page), F32),
                        pltpu.VMEM((n_pages * page, KV_DIM), F32), pltpu.SemaphoreType.DMA((2, 2))])
    return pl.pallas_call(
        functools.partial(_sample_compress_kernel, n_pages=n_pages, page=page),
        grid_spec=grid_spec, out_shape=[out_sd, out_sd],
        compiler_params=pltpu.CompilerParams(dimension_semantics=("arbitrary",),
                                             vmem_limit_bytes=VMEM_LIMIT),
        name="compress_sample",
    )(page_table, kpool_t, vpool_t, w1k, w2k, w1v, w2v, pe_rows, kg, p128, cos, sa, sb)


def _query_rows(q, kvh):
    lane = lax.broadcasted_iota(jnp.int32, (1, LANES), 1)
    keep = (lane < HEAD_DIM) if kvh == 0 else (lane >= HEAD_DIM)
    return jnp.concatenate(
        [jnp.where(keep, q[:, g * LANES:(g + 1) * LANES], jnp.zeros((), q.dtype)) for g in range(GQA_GROUP)],
        axis=0)


def _group_sum(p, nq):
    return p[0:nq] + p[nq:2 * nq] + p[2 * nq:3 * nq] + p[3 * nq:4 * nq]


def _importance(psum, ov):
    hi = psum.astype(BF16)
    lo = (psum - hi.astype(F32)).astype(BF16)
    return _dot(hi, ov) + _dot(lo, ov)


def _select_blocks(imp, qpos, n_blocks):
    w = imp.shape[1]
    blk = lax.broadcasted_iota(jnp.int32, (1, w), 1)
    cur = qpos >> 6
    forced = (blk == 0) | (blk == cur) | (blk == cur - 1)
    causal = blk * SEL_BLOCK <= qpos
    score = jnp.where(causal, imp + SEL_BONUS * forced.astype(F32), NEG_INF)
    score = jnp.where(blk < n_blocks, score, PAD_SCORE)
    rank = jnp.zeros(score.shape, F32)
    for j in range(n_blocks):
        col = score[:, j:j + 1]
        beats = (col > score) | ((col == score) & (blk > j))
        rank = rank + jnp.where(beats, 1.0, 0.0)
    return jnp.where(rank < float(min(N_SEL, n_blocks)), 1.0, 0.0)


def _select_blocks_t(imp_t, qpos, n_blocks):
    assert imp_t.shape[0] == n_blocks
    blk = lax.broadcasted_iota(jnp.int32, (n_blocks, 1), 0)
    cur = qpos >> 6
    forced = (blk == 0) | (blk == cur) | (blk == cur - 1)
    causal = blk * SEL_BLOCK <= qpos
    score = jnp.where(causal, imp_t + SEL_BONUS * forced.astype(F32), NEG_INF)
    rank = jnp.zeros(score.shape, F32)
    for j in range(n_blocks):
        row = score[j:j + 1, :]
        beats = (row > score) | ((row == score) & (blk > j))
        rank = rank + jnp.where(beats, 1.0, 0.0)
    return jnp.where(rank < float(min(N_SEL, n_blocks)), 1.0, 0.0)


def _gate_lanes(gates, ex):
    hi = gates.astype(BF16)
    lo = (gates - hi.astype(F32)).astype(BF16)
    return _dot(jnp.concatenate([hi, lo], axis=1), ex)


def _gate_rows(gates, kvh, branch):
    cols = [(kvh * GQA_GROUP + g) * N_BRANCH + branch for g in range(GQA_GROUP)]
    return jnp.concatenate([gates[:, c:c + 1] for c in cols], axis=0)


def _online_update(state, s, v):
    m, l, acc = state
    m_new = jnp.maximum(m, jnp.max(s, axis=-1, keepdims=True))
    alpha = jnp.exp2(m - m_new)
    e = jnp.exp2(s - m_new)
    l = alpha * l + jnp.sum(e, axis=-1, keepdims=True)
    acc = alpha * acc + _dot(e.astype(BF16), v)
    return m_new, l, acc


def _online_init(rows):
    return (jnp.full((rows, 1), NEG_INF, F32), jnp.zeros((rows, 1), F32), jnp.zeros((rows, KV_DIM), F32))


def _online_finish(state):
    _, l, acc = state
    return acc / jnp.maximum(l, 1e-30)


def _block_bias(sel):
    return jnp.where(sel > 0.5, 0.0, MASK_BIAS).astype(BF16)


def _merge_heads(o_ref, outs, nq):
    lane = lax.broadcasted_iota(jnp.int32, (1, LANES), 1)
    for g in range(GQA_GROUP):
        o_ref[:, g * LANES:(g + 1) * LANES] = jnp.where(lane < HEAD_DIM, outs[0][g * nq:(g + 1) * nq],
                                                         outs[1][g * nq:(g + 1) * nq])


def _prompt_attn_kernel(q_ref, gate_ref, kc_ref, vc_ref, ks_ref, vs_ref, kw_ref, vw_ref, ovt_ref,
                        et_ref, ex_ref, o_ref, *, TQ, TK, n_cmp, n_blocks):
    i = pl.program_id(1)
    R = GQA_GROUP * TQ
    s0 = i * TQ
    qpos = s0 + (lax.broadcasted_iota(jnp.int32, (R, 1), 0) & (TQ - 1))
    qpos_l = s0 + lax.broadcasted_iota(jnp.int32, (1, TQ), 1)
    n_idx = lax.broadcasted_iota(jnp.int32, (1, kc_ref.shape[0]), 1)
    blk_r = lax.broadcasted_iota(jnp.int32, (n_blocks, 1), 0)
    n_full = s0 // TK
    wspan = WINDOW + TQ
    wstart = pl.multiple_of(jnp.maximum(s0 - WINDOW, 0), TQ)
    wdist = qpos - (wstart + lax.broadcasted_iota(jnp.int32, (1, wspan), 1))
    wmask = (wdist >= 0) & (wdist < WINDOW)
    q = q_ref[...]
    gx = _gate_lanes(gate_ref[...], ex_ref[...])
    pad_blocks = jnp.zeros((LANES - n_blocks, TQ), F32)
    outs = []
    for kvh in range(N_KV_HEADS):
        qz = _query_rows(q, kvh)
        cmask = (n_idx < n_cmp) & (n_idx * CMP_STRIDE + (CMP_LEN - 1) <= qpos)
        p = _masked_softmax2(_dot_t(qz, kc_ref[...]), cmask)
        o_cmp = _dot(p.astype(BF16), vc_ref[...])

        def ranked():
            psum = _group_sum(p, TQ)
            hi = psum.astype(BF16)
            lo = (psum - hi.astype(F32)).astype(BF16)
            imp_t = _dot_t(ovt_ref[...], hi) + _dot_t(ovt_ref[...], lo)
            return _select_blocks_t(imp_t[0:n_blocks], qpos_l, n_blocks)

        sel_t = lax.cond(s0 + TQ <= N_SEL * SEL_BLOCK,
                         lambda: jnp.where(blk_r * SEL_BLOCK <= qpos_l, 1.0, 0.0), ranked)
        bias = jnp.concatenate([jnp.where(sel_t > 0.5, 0.0, MASK_BIAS), pad_blocks], axis=0).T.astype(BF16)

        lhs = jnp.concatenate([qz, jnp.concatenate([bias] * GQA_GROUP, axis=0)], axis=1)

        def tile(kt):
            start = pl.multiple_of(kt * TK, TK)
            k_aug = jnp.concatenate([ks_ref[pl.ds(start, TK), :], et_ref[pl.ds(start, TK), :]], axis=1)
            return _dot_t(lhs, k_aug), vs_ref[pl.ds(start, TK), :]

        state = lax.fori_loop(0, n_full, lambda kt, st: _online_update(st, *tile(kt)), _online_init(R))
        s_last, v_last = tile(n_full)
        causal = n_full * TK + lax.broadcasted_iota(jnp.int32, (1, TK), 1) <= qpos
        o_slc = _online_finish(_online_update(state, jnp.where(causal, s_last, MASK_BIAS), v_last))

        sw = jnp.where(wmask, _dot_t(qz, kw_ref[pl.ds(wstart, wspan), :]), MASK_BIAS)
        ew = jnp.exp2(sw - jnp.max(sw, axis=-1, keepdims=True))
        o_win = _dot(ew.astype(BF16), vw_ref[pl.ds(wstart, wspan), :]) / jnp.sum(ew, axis=-1, keepdims=True)
        branches = (o_cmp, o_slc, o_win)
        outs.append([sum(br[g * TQ:(g + 1) * TQ] * gx[:, c * D_ATTN + g * LANES:c * D_ATTN + (g + 1) * LANES]
                         for c, br in enumerate(branches)) for g in range(GQA_GROUP)])
    lane = lax.broadcasted_iota(jnp.int32, (1, LANES), 1)
    for g in range(GQA_GROUP):
        o_ref[:, g * LANES:(g + 1) * LANES] = jnp.where(lane < HEAD_DIM, outs[0][g], outs[1][g])


def _gate_expander():
    ex = np.zeros((LANES, N_BRANCH * D_ATTN), np.float32)
    for kvh in range(N_KV_HEADS):
        for g in range(GQA_GROUP):
            for c in range(N_BRANCH):
                lo = c * D_ATTN + g * LANES + kvh * HEAD_DIM
                ex[(kvh * GQA_GROUP + g) * N_BRANCH + c, lo:lo + HEAD_DIM] = 1.0
    return jnp.asarray(np.concatenate([ex, ex], axis=0), BF16)


def _prompt_attn_call(q, gates, kc, vc, ksb, vsb, kwb, vwb, n_batch):
    T = q.shape[0] // n_batch
    TQ, TK = 128, 512
    n_qt = T // TQ
    n_ch = T // CMP_STRIDE
    n_cmp = n_ch - CMP_LEN // CMP_STRIDE + 1
    n_blocks = -(-T // SEL_BLOCK)
    assert n_ch <= LANES and n_blocks <= LANES and T % TK == 0 and TK % TQ == 0 and WINDOW % TQ == 0
    assert WINDOW + TQ <= T and (N_SEL * SEL_BLOCK) % TQ == 0
    n = np.arange(n_ch)[:, None]
    j = np.arange(LANES)[None, :]
    ov = ((n * CMP_STRIDE < (j + 1) * SEL_BLOCK) & (n * CMP_STRIDE + CMP_LEN > j * SEL_BLOCK)
          & (n < n_cmp) & (j < n_blocks))
    ovt = jnp.asarray(ov.T.astype(np.float32), BF16)
    et = jnp.asarray((np.arange(T)[:, None] // SEL_BLOCK == np.arange(LANES)[None, :]).astype(np.float32),
                     BF16)
    ex = _gate_expander()
    assert n_blocks % 8 == 0
    qtile = lambda w: pl.BlockSpec((TQ, w), lambda b, i: (b * n_qt + i, 0))
    perb = lambda r: pl.BlockSpec((r, KV_DIM), lambda b, i: (b, 0))
    const2 = lambda a: pl.BlockSpec(a.shape, lambda b, i: (0, 0))
    return pl.pallas_call(
        functools.partial(_prompt_attn_kernel, TQ=TQ, TK=TK, n_cmp=n_cmp, n_blocks=n_blocks),
        grid=(n_batch, n_qt),
        in_specs=[qtile(D_ATTN), qtile(LANES), perb(n_ch), perb(n_ch), perb(T), perb(T), perb(T), perb(T),
                  const2(ovt), const2(et), const2(ex)],
        out_specs=qtile(D_ATTN),
        out_shape=jax.ShapeDtypeStruct((n_batch * T, D_ATTN), F32),
        compiler_params=pltpu.CompilerParams(dimension_semantics=("arbitrary", "arbitrary"),
                                             vmem_limit_bytes=VMEM_LIMIT),
        name="attn_prompt",
    )(q, gates, kc, vc, ksb, vsb, kwb, vwb, ovt, et, ex)


def _sample_attn_kernel(pt_ref, q_ref, gate_ref, kc_ref, vc_ref, kpool, vpool, ksn_ref, vsn_ref,
                        kws_ref, vws_ref, kwn_ref, vwn_ref, ov_ref, eexp_ref, o_ref, kwo_ref, vwo_ref,
                        kbuf, vbuf, kaug, sem, *, n_pages, page, n_cmp, n_blocks, past, ts):
    R = GQA_GROUP * ts
    n_past_blocks = past // SEL_BLOCK
    wbuf = kws_ref.shape[2]
    slot = _page_gather(pt_ref, (kpool, vpool), (kbuf, vbuf), sem,
                        lambda buf, s, p: buf.at[s, :, pl.ds(pl.multiple_of(p * page, page), page)], n_pages)

    @pl.when(pl.program_id(0) == 0)
    def _():
        kaug[KV_DIM:2 * KV_DIM, :] = eexp_ref[...]
    kaug[0:KV_DIM, :] = kbuf[slot].astype(BF16)
    v_past = vbuf[slot].astype(BF16)

    q = q_ref[0]
    gates = gate_ref[0]
    qz = jnp.concatenate([_query_rows(q, 0), _query_rows(q, 1)], axis=0)
    qpos = past + (lax.broadcasted_iota(jnp.int32, (2 * R, 1), 0) & (ts - 1))
    qpos_t = past + (lax.broadcasted_iota(jnp.int32, (N_KV_HEADS * ts, 1), 0) & (ts - 1))
    lane = lax.broadcasted_iota(jnp.int32, (1, LANES), 1)
    n_idx = lax.broadcasted_iota(jnp.int32, (1, kc_ref.shape[1]), 1)
    pad_rows = jnp.zeros((LANES - ts, KV_DIM), F32)
    new_tile = lambda ref: jnp.concatenate([ref[0], pad_rows], axis=0).astype(BF16)
    dist_new = qpos - (past + lane)
    new_valid = (lane < ts) & (dist_new >= 0)

    def per_group(x):
        return jnp.concatenate([x[0:ts]] * GQA_GROUP + [x[ts:2 * ts]] * GQA_GROUP, axis=0)

    def two_part_attention(s_old, v_t, s_new, v_new):
        m = jnp.maximum(jnp.max(s_old, axis=-1, keepdims=True), jnp.max(s_new, axis=-1, keepdims=True))
        eo = jnp.exp2(s_old - m)
        en = jnp.exp2(s_new - m)
        l = jnp.sum(eo, axis=-1, keepdims=True) + jnp.sum(en, axis=-1, keepdims=True)
        return (_dot_t(eo.astype(BF16), v_t) + _dot(en.astype(BF16), v_new)) / l

    cmask = (n_idx < n_cmp) & (n_idx * CMP_STRIDE + (CMP_LEN - 1) <= qpos)
    p = _masked_softmax2(_dot_t(qz, kc_ref[0]), cmask)
    o_cmp = _dot(p.astype(BF16), vc_ref[0])
    psum = jnp.concatenate([_group_sum(p[0:R], ts), _group_sum(p[R:2 * R], ts)], axis=0)
    sel = _select_blocks(_importance(psum, ov_ref[...]), qpos_t, n_blocks)
    bias = per_group(jnp.where(sel > 0.5, 0.0, MASK_BIAS))

    lhs = jnp.concatenate([qz, bias[:, 0:n_past_blocks].astype(BF16)], axis=1)
    s_new = jnp.where(new_valid, _dot_t(qz, new_tile(ksn_ref)) + bias[:, n_past_blocks:n_past_blocks + 1],
                      MASK_BIAS)
    o_slc = two_part_attention(_dot(lhs, kaug[...]), v_past, s_new, new_tile(vsn_ref))

    kw_state = kws_ref[0]
    vw_state = vws_ref[0]
    wpos = past - wbuf + lax.broadcasted_iota(jnp.int32, (1, wbuf), 1)
    wdist = qpos - wpos
    s_w = jnp.where((wdist >= 0) & (wdist < WINDOW) & (wpos >= 0), _dot(qz, kw_state.astype(BF16)), MASK_BIAS)
    s_wn = jnp.where(new_valid & (dist_new < WINDOW), _dot_t(qz, new_tile(kwn_ref)), MASK_BIAS)
    o_win = two_part_attention(s_w, vw_state.astype(BF16), s_wn, new_tile(vwn_ref))

    gate = lambda c: jnp.concatenate([_gate_rows(gates, kvh, c) for kvh in range(N_KV_HEADS)], axis=0)
    o = o_cmp * gate(0) + o_slc * gate(1) + o_win * gate(2)
    for g in range(GQA_GROUP):
        o_ref[0, :, g * LANES:(g + 1) * LANES] = jnp.where(
            lane < HEAD_DIM, o[g * ts:(g + 1) * ts], o[R + g * ts:R + (g + 1) * ts])

    def shifted(state, new_ref, out_ref):
        rolled = pltpu.roll(state, wbuf - ts, 1)
        new_t = pltpu.roll(jnp.concatenate([new_ref[0], pad_rows], axis=0).T, LANES - ts, 1)
        out_ref[0, :, 0:wbuf - LANES] = rolled[:, 0:wbuf - LANES]
        out_ref[0, :, wbuf - LANES:wbuf] = jnp.where(lane >= LANES - ts, new_t, rolled[:, wbuf - LANES:wbuf])

    shifted(kw_state, kwn_ref, kwo_ref)
    shifted(vw_state, vwn_ref, vwo_ref)


def _sample_attn_call(page_table, q3, gates3, kc, vc, kpool_t, vpool_t, ks_new, vs_new, kw_state_t, vw_state_t,
                      kw_new, vw_new):
    n_batch, n_pages = page_table.shape
    page = kpool_t.shape[2]
    ts = q3.shape[1]
    past = n_pages * page
    n_ch = kc.shape[1]
    n_cmp = n_ch - CMP_LEN // CMP_STRIDE + 1
    n_blocks = -(-(past + ts) // SEL_BLOCK)
    wbuf = kw_state_t.shape[2]
    assert past % SEL_BLOCK == 0 and ts <= SEL_BLOCK and past // SEL_BLOCK == LANES and wbuf > LANES
    w_sel = 2 * LANES
    n = np.arange(n_ch)[:, None]
    j = np.arange(w_sel)[None, :]
    ov = ((n * CMP_STRIDE < (j + 1) * SEL_BLOCK) & (n * CMP_STRIDE + CMP_LEN > j * SEL_BLOCK)
          & (n < n_cmp) & (j < n_blocks))
    ov = jnp.asarray(ov.astype(np.float32), BF16)
    eexp = jnp.asarray((np.arange(past)[None, :] // SEL_BLOCK == np.arange(LANES)[:, None]).astype(np.float32),
                       BF16)
    perb = lambda a: pl.BlockSpec((1,) + a.shape[1:], lambda b, pt: (b, 0, 0))
    const2 = lambda a: pl.BlockSpec(a.shape, lambda b, pt: (0, 0))
    anyspec = pl.BlockSpec(memory_space=pl.ANY)
    grid_spec = pltpu.PrefetchScalarGridSpec(
        num_scalar_prefetch=1, grid=(n_batch,),
        in_specs=[perb(q3), perb(gates3), perb(kc), perb(vc), anyspec, anyspec, perb(ks_new), perb(vs_new),
                  perb(kw_state_t), perb(vw_state_t), perb(kw_new), perb(vw_new), const2(ov), const2(eexp)],
        out_specs=[pl.BlockSpec((1, ts, D_ATTN), lambda b, pt: (b, 0, 0)), perb(kw_state_t), perb(vw_state_t)],
        scratch_shapes=[pltpu.VMEM((2, KV_DIM, past), F32), pltpu.VMEM((2, KV_DIM, past), F32),
                        pltpu.VMEM((2 * KV_DIM, past), BF16), pltpu.SemaphoreType.DMA((2, 2))])
    state_sd = jax.ShapeDtypeStruct(kw_state_t.shape, F32)
    return pl.pallas_call(
        functools.partial(_sample_attn_kernel, n_pages=n_pages, page=page, n_cmp=n_cmp, n_blocks=n_blocks,
                          past=past, ts=ts),
        grid_spec=grid_spec,
        out_shape=[jax.ShapeDtypeStruct((n_batch, ts, D_ATTN), F32), state_sd, state_sd],
        compiler_params=pltpu.CompilerParams(dimension_semantics=("arbitrary",),
                                             vmem_limit_bytes=VMEM_LIMIT),
        name="attn_sample",
    )(page_table, q3, gates3, kc, vc, kpool_t, vpool_t, ks_new, vs_new, kw_state_t, vw_state_t, kw_new, vw_new,
      ov, eexp)


def _outffn_kernel(x_ref, o_ref, v_ref, g1_ref, sh2_ref, sc2_ref, g2_ref, ag_ref, ln2_ref, woa_ref, wov_ref,
                   wgu_ref, wd_ref, y_ref, *, G, L, ff_chunk):
    M = G * L

    def expand(ref):
        v = ref[...]
        return jnp.broadcast_to(v, (G, L, v.shape[-1])).reshape(M, v.shape[-1])

    a = (_rms(o_ref[...]) * ag_ref[...]).astype(BF16)
    mix = _dot(a, woa_ref[...]) + _dot(v_ref[...], wov_ref[...])
    x1 = x_ref[...] + expand(g1_ref) * mix
    h2 = (_rms(x1) * ln2_ref[...] * (1.0 + expand(sc2_ref)) + expand(sh2_ref)).astype(BF16)
    y = jnp.zeros((M, D_MODEL), F32)
    for c in range(D_FF // ff_chunk):
        lo = c * ff_chunk
        g = _dot(h2, wgu_ref[:, lo:lo + ff_chunk])
        u = _dot(h2, wgu_ref[:, D_FF + lo:D_FF + lo + ff_chunk])
        y = y + _dot((_silu(g) * u).astype(BF16), wd_ref[lo:lo + ff_chunk, :])
    y_ref[...] = x1 + expand(g2_ref) * y


def _outffn_call(x2, o2, v2, mod3, mod_off, G, L, tiles_per_group, wts, name):
    ntok = x2.shape[0]
    M = G * L
    n_tiles = ntok // M
    ag, ln2, woa, wov, wgu, wd = wts
    mod_idx = (lambda i: i) if G > 1 else (lambda i: mod_off + i // tiles_per_group)
    tok = lambda n: pl.BlockSpec((M, n), lambda i: (i, 0))
    modspec = lambda col: pl.BlockSpec((G, 1, D_MODEL), lambda i: (mod_idx(i), 0, col))
    const2 = lambda a: pl.BlockSpec(a.shape, lambda i: (0, 0))
    return pl.pallas_call(
        functools.partial(_outffn_kernel, G=G, L=L, ff_chunk=D_FF // 2),
        grid=(n_tiles,),
        in_specs=[tok(D_MODEL), tok(D_ATTN), tok(D_CONV), modspec(2), modspec(3), modspec(4), modspec(5),
                  const2(ag), const2(ln2), const2(woa), const2(wov), const2(wgu), const2(wd)],
        out_specs=tok(D_MODEL),
        out_shape=jax.ShapeDtypeStruct((ntok, D_MODEL), F32),
        compiler_params=pltpu.CompilerParams(dimension_semantics=("arbitrary",),
                                             vmem_limit_bytes=VMEM_LIMIT),
        name=name,
    )(x2, o2, v2, mod3, mod3, mod3, mod3, ag, ln2, woa, wov, wgu, wd)


def _rope_tables(pos):
    half = ROT_DIM // 2
    n = pos.shape[0]
    inv = ROPE_THETA ** (-jnp.arange(half, dtype=F32) * 2.0 / ROT_DIM)
    ang = pos.astype(F32)[:, None] * inv[None, :]
    cos, sin = jnp.cos(ang), jnp.sin(ang)
    rest = HEAD_DIM - ROT_DIM
    z8 = jnp.zeros((n, half), F32)
    c64 = jnp.concatenate([cos, cos, jnp.ones((n, rest), F32)], axis=1)
    a64 = jnp.concatenate([z8, sin, jnp.zeros((n, rest), F32)], axis=1)
    b64 = jnp.concatenate([-sin, z8, jnp.zeros((n, rest), F32)], axis=1)
    two = lambda a: jnp.concatenate([a, a], axis=1)
    return two(c64), two(a64), two(b64)


def _head_perm_cols(w):
    lead = w.shape[:-1]
    return w.reshape(lead + (N_KV_HEADS, GQA_GROUP, HEAD_DIM)).swapaxes(-3, -2).reshape(lead + (D_ATTN,))


def _compress_weights(w1, w2):
    r_n = CMP_LEN // CMP_STRIDE
    w1r = w1.reshape(r_n, CMP_STRIDE, HEAD_DIM, HEAD_DIM)
    eye = jnp.eye(N_KV_HEADS, dtype=F32)
    w1big = jnp.einsum('rpdh,kl->pkdrlh', w1r, eye).reshape(CMP_STRIDE * KV_DIM, r_n * KV_DIM)
    w2big = jnp.einsum('dh,kl->kdlh', w2, eye).reshape(KV_DIM, KV_DIM)
    return w1big.astype(BF16), w2big.astype(BF16)


def kernel(x_prompt, x_sample, cache_k_cmp, cache_v_cmp, cache_k_slc, cache_v_slc, state_k_win, state_v_win,
           state_conv, page_table, c_prompt, c_sample, w_ada, b_ada, ln1_g, w_in, q_norm_g, k_norm_g, pe_cmp,
           w_ck1, w_ck2, w_cv1, w_cv2, w_conv, attn_out_g, conv_out_g, w_o, ln2_g, w_gu, w_down):
    depth = w_in.shape[0]
    bp, tp, _ = x_prompt.shape
    bs, ts, _ = x_sample.shape
    past = page_table.shape[1] * cache_k_cmp.shape[2]
    wbuf = state_k_win.shape[2]
    assert wbuf == WINDOW and tp >= WINDOW and ts == 8 and bs % 64 == 0

    tab_p = tuple(a[None] for a in _rope_tables(jnp.arange(tp)))
    tab_s = tuple(a[None] for a in _rope_tables(past + jnp.arange(ts)))
    end_p = jnp.arange(tp // CMP_STRIDE) * CMP_STRIDE + CMP_LEN - 1
    end_s = jnp.arange(past // CMP_STRIDE) * CMP_STRIDE + CMP_LEN - 1
    tab_cp = _rope_tables(end_p)
    tab_cs = _rope_tables(end_s)
    p128 = jnp.asarray(np.kron(np.eye(2), np.full((HEAD_DIM, HEAD_DIM), 1.0 / HEAD_DIM)), BF16)

    xp = x_prompt.reshape(bp * tp, D_MODEL)
    xs = x_sample.reshape(bs * ts, D_MODEL)
    c_all = jnp.concatenate([c_sample, c_prompt], axis=0)
    heads5 = lambda a, b, t: a.reshape(b, t, N_KV_HEADS, HEAD_DIM)
    to_t = lambda a: a.transpose(0, 2, 3, 1).reshape(a.shape[0], KV_DIM, a.shape[1])
    from_t = lambda a: a.reshape(a.shape[0], N_KV_HEADS, HEAD_DIM, a.shape[2]).transpose(0, 3, 1, 2)
    per_layer = []
    for l in range(depth):
        wl = w_in[l]
        o_kv = D_ATTN
        o_g = o_kv + 6 * KV_DIM
        o_c = o_g + N_BRANCH * N_HEADS
        w_perm = jnp.concatenate(
            [_head_perm_cols(wl[:, :D_ATTN]), wl[:, o_kv:o_g], wl[:, o_c:o_c + 3 * D_CONV], wl[:, o_g:o_c],
             jnp.zeros((D_MODEL, LANES - N_BRANCH * N_HEADS), F32)], axis=1).astype(BF16)
        qg = jnp.tile(q_norm_g[l], N_HEADS)[None]
        kg = [jnp.tile(k_norm_g[l, i], N_KV_HEADS)[None] for i in range(N_BRANCH)]
        in_wts = (w_perm, qg, kg[1], kg[2], p128, w_conv[l], conv_out_g[l][None], ln1_g[l][None])
        pe_rows = jnp.broadcast_to(pe_cmp[l].reshape(CMP_LEN // CMP_STRIDE, CMP_STRIDE, 1, HEAD_DIM),
                                   (CMP_LEN // CMP_STRIDE, CMP_STRIDE, N_KV_HEADS, HEAD_DIM))
        pe_rows = jnp.concatenate([pe_rows.reshape(CMP_LEN // CMP_STRIDE, CMP_STRIDE * KV_DIM),
                                   jnp.zeros((8 - CMP_LEN // CMP_STRIDE, CMP_STRIDE * KV_DIM), F32)],
                                  axis=0).astype(BF16)
        cw = _compress_weights(w_ck1[l], w_ck2[l]) + _compress_weights(w_cv1[l], w_cv2[l]) + (pe_rows, kg[0], p128)
        out_wts = (_head_perm_cols(attn_out_g[l])[None], ln2_g[l][None],
                   _head_perm_cols(w_o[l][:D_ATTN].T).T.astype(BF16), w_o[l][D_ATTN:].astype(BF16),
                   w_gu[l].astype(BF16), w_down[l].astype(BF16))

        mod3 = _adaln_call(c_all, w_ada[l], b_ada[l]).reshape(bs + bp, 1, 6 * D_MODEL)

        tm = 512
        (q, kc_t, vc_t, ks_t, vs_t, kw_t, vw_t, kcb, vcb, ksb, vsb, kwb, vwb, gates, vconv, p_conv_l) = _inproj_call(
            xp, mod3, bs, 1, tm, tp // tm, tab_p, None, in_wts)
        kc, vc = _prompt_compress_call(kcb, vcb, bp, cw, tab_cp)
        o = _prompt_attn_call(q, gates, kc, vc, ksb, vsb, kwb, vwb, bp)
        xp = _outffn_call(xp, o, vconv, mod3, bs, 1, tm, tp // tm, out_wts, "outffn_prompt")
        prompt_state = tuple(from_t(a) for a in (kc_t, vc_t, ks_t, vs_t, kw_t[:, :, tp - WINDOW:],
                                                 vw_t[:, :, tp - WINDOW:])) + (p_conv_l,)

        gs = 64
        (q, kc_r, vc_r, ks, vs, kw, vw, _, _, _, _, _, _, gates, vconv, s_conv_l) = _inproj_call(
            xs, mod3, 0, gs, ts, 1, tab_s, state_conv[l], in_wts)
        kc, vc = _sample_compress_call(page_table, to_t(cache_k_cmp[l]), to_t(cache_v_cmp[l]), cw, tab_cs)
        r3 = lambda a: a.reshape(bs, ts, a.shape[-1])
        o, k_win_t, v_win_t = _sample_attn_call(
            page_table, r3(q), r3(gates), kc, vc, to_t(cache_k_slc[l]), to_t(cache_v_slc[l]), r3(ks), r3(vs),
            to_t(state_k_win[l]), to_t(state_v_win[l]), r3(kw), r3(vw))
        xs = _outffn_call(xs, o.reshape(bs * ts, D_ATTN), vconv, mod3, 0, gs, ts, 1, out_wts, "outffn_sample")
        sample_state = (heads5(kc_r, bs, ts), heads5(vc_r, bs, ts), heads5(ks, bs, ts), heads5(vs, bs, ts),
                        from_t(k_win_t), from_t(v_win_t), s_conv_l)
        per_layer.append(prompt_state + sample_state)

    states = [jnp.stack(z) for z in zip(*per_layer)]
    return (xp.reshape(bp, tp, D_MODEL), xs.reshape(bs, ts, D_MODEL)) + tuple(states)
```
